```python
import math
import jax, jax.numpy as jnp
from jax import lax
import numpy as np

D_MODEL = 1024
BATCH = 4
SEQ = 4096
DEPTH = 1

PLE_DIM = 256
DN_HEADS = 8
DN_DK = 128
DN_DV = 128
DN_CONV = 4
DN_CHUNK = 64
DN_QK_W = DN_HEADS * DN_DK
DN_V_W = DN_HEADS * DN_DV
DN_CONV_CH = 2 * DN_QK_W + DN_V_W
MLA_HEADS = 8
MLA_Q_LORA = 384
MLA_KV_LORA = 256
MLA_NOPE = 128
MLA_ROPE = 64
MLA_V = 128
MLA_V_W = MLA_HEADS * MLA_V
ROPE_BASE = 10000.0
Q_BLOCK = 128
FFN_HIDDEN = -(-8 * D_MODEL // (3 * 256)) * 256
DEEPNORM_ALPHA = (2.0 * DEPTH) ** 0.25
DEEPNORM_BETA = (8.0 * DEPTH) ** -0.25
IN_SIZES = (DN_CONV_CH, DN_V_W, DN_HEADS, DN_HEADS, MLA_Q_LORA, MLA_KV_LORA, MLA_ROPE, D_MODEL, D_MODEL)
D_IN = sum(IN_SIZES)
SPLIT_IDX = tuple(int(v) for v in np.cumsum(IN_SIZES)[:-1])
NEG_BIG = -1e30

kernel_name = 'hybrid_deltanet_mla_deepnorm_block'


def layer_norm(t, g, b, eps=1e-5):
    tf = t.astype(jnp.float32)
    mu = jnp.mean(tf, axis=-1, keepdims=True)
    var = jnp.mean(jnp.square(tf - mu), axis=-1, keepdims=True)
    return ((tf - mu) * lax.rsqrt(var + eps) * g.astype(jnp.float32) + b.astype(jnp.float32)).astype(t.dtype)


def rms_norm(t, w, eps=1e-6):
    tf = t.astype(jnp.float32)
    return tf * lax.rsqrt(jnp.mean(jnp.square(tf), axis=-1, keepdims=True) + eps) * w.astype(jnp.float32)


def l2_normalize(t, eps=1e-6):
    tf = t.astype(jnp.float32)
    return tf * lax.rsqrt(jnp.sum(jnp.square(tf), axis=-1, keepdims=True) + eps)


def rope_tables(positions):
    inv_freq = ROPE_BASE ** (-jnp.arange(0, MLA_ROPE, 2, dtype=jnp.float32) / MLA_ROPE)
    ang = positions.astype(jnp.float32)[..., None] * inv_freq
    return jnp.cos(ang), jnp.sin(ang)


def apply_rope(t, cos, sin):
    t1, t2 = jnp.split(t.astype(jnp.float32), 2, axis=-1)
    return jnp.concatenate([t1 * cos - t2 * sin, t2 * cos + t1 * sin], axis=-1)


def causal_depthwise_conv(t, w):
    width, ch = w.shape
    return lax.conv_general_dilated(t, w[:, None, :].astype(t.dtype), window_strides=(1,), padding=[(width - 1, 0)], dimension_numbers=('NWC', 'WIO', 'NWC'), feature_group_count=ch)


def gated_delta_rule(q, k, v, beta, g):
    b, s, h, dk = q.shape
    dv = v.shape[-1]
    c = DN_CHUNK
    n = s // c

    def chunk(t):
        return jnp.swapaxes(t.reshape((b, n, c) + t.shape[2:]), 2, 3)

    q, k, v, beta, g = (chunk(t) for t in (q, k, v, beta, g))
    g = jnp.cumsum(g, axis=-1)
    tril = jnp.tril(jnp.ones((c, c), dtype=bool))
    strict = jnp.tril(jnp.ones((c, c), dtype=bool), k=-1)
    diff = g[..., :, None] - g[..., None, :]
    decay = jnp.where(tril, jnp.exp(jnp.where(tril, diff, 0.0)), 0.0)
    k_beta = k * beta[..., None]
    l_mat = jnp.where(strict, jnp.einsum('bnhid,bnhjd->bnhij', k_beta, k) * decay, 0.0)
    eye = jnp.eye(c, dtype=q.dtype)
    t_inv = lax.linalg.triangular_solve(eye + l_mat, jnp.broadcast_to(eye, l_mat.shape), left_side=True, lower=True, unit_diagonal=True)
    u = jnp.einsum('bnhij,bnhje->bnhie', t_inv, v * beta[..., None])
    w = jnp.einsum('bnhij,bnhjd->bnhid', t_inv, k_beta * jnp.exp(g)[..., None])
    intra = jnp.where(tril, jnp.einsum('bnhid,bnhjd->bnhij', q, k) * decay, 0.0)
    q_dec = q * jnp.exp(g)[..., None]
    g_last = g[..., -1]
    k_tail = k * jnp.exp(g_last[..., None] - g)[..., None]

    def step(state, xs):
        w_c, u_c, q_c, a_c, kt_c, gl_c = xs
        v_new = u_c - jnp.einsum('bhcd,bhde->bhce', w_c, state)
        o_c = jnp.einsum('bhcd,bhde->bhce', q_c, state) + jnp.einsum('bhij,bhje->bhie', a_c, v_new)
        state = state * jnp.exp(gl_c)[..., None, None] + jnp.einsum('bhcd,bhce->bhde', kt_c, v_new)
        return state, o_c

    xs = tuple(jnp.moveaxis(t, 1, 0) for t in (w, u, q_dec, intra, k_tail, g_last))
    state0 = jnp.zeros((b, h, dk, dv), q.dtype)
    _, o = lax.scan(step, state0, xs)
    return jnp.transpose(o, (1, 0, 3, 2, 4)).reshape(b, s, h, dv)


def mla_attention(q_lat, q_rope, c_kv, k_rope):
    b, s, h, c = q_lat.shape
    nblk = s // Q_BLOCK
    scale = (MLA_NOPE + MLA_ROPE) ** -0.5
    ckv = c_kv.astype(jnp.float32)
    kr = k_rope.astype(jnp.float32)
    key_idx = jnp.arange(s)

    def blocks(t):
        return jnp.swapaxes(t.reshape((b, nblk, Q_BLOCK) + t.shape[2:]), 0, 1)

    def one_block(args):
        ql, qr, blk = args
        sc = (jnp.einsum('bqhc,bkc->bhqk', ql, ckv) + jnp.einsum('bqhr,bkr->bhqk', qr, kr)) * scale
        q_idx = blk * Q_BLOCK + jnp.arange(Q_BLOCK)
        sc = jnp.where(key_idx[None, :] <= q_idx[:, None], sc, NEG_BIG)
        pr = jax.nn.softmax(sc, axis=-1)
        return jnp.einsum('bhqk,bkc->bqhc', pr, ckv)

    out = lax.map(one_block, (blocks(q_lat.astype(jnp.float32)), blocks(q_rope.astype(jnp.float32)), jnp.arange(nblk)))
    return jnp.swapaxes(out, 0, 1).reshape(b, s, h, c)


def setup_inputs(seed: int = 0) -> dict:
    key = jax.random.key(seed)
    ks = jax.random.split(key, 32)

    def nrm(k, shape, scale):
        return jax.random.normal(k, shape, jnp.float32) * scale

    x = nrm(ks[0], (BATCH, SEQ, D_MODEL), 1.0)
    p = nrm(ks[1], (DEPTH, BATCH, SEQ, PLE_DIM), 1.0)
    positions = jax.random.randint(ks[2], (BATCH, 1), 0, 1024, dtype=jnp.int32) + jnp.arange(SEQ, dtype=jnp.int32)[None, :]
    w_in = nrm(ks[3], (DEPTH, D_MODEL, D_IN), D_MODEL ** -0.5)
    conv_w = nrm(ks[4], (DEPTH, DN_CONV, DN_CONV_CH), DN_CONV ** -0.5)
    dn_a_log = jnp.log(jax.random.uniform(ks[5], (DEPTH, DN_HEADS), jnp.float32, 1.0, 16.0))
    dt = jnp.exp(jax.random.uniform(ks[6], (DEPTH, DN_HEADS), jnp.float32, math.log(1e-3), math.log(1e-1)))
    dn_dt_bias = dt + jnp.log(-jnp.expm1(-dt))
    dn_norm_w = 1.0 + nrm(ks[7], (DEPTH, DN_DV), 0.01)
    q_norm_w = 1.0 + nrm(ks[8], (DEPTH, MLA_Q_LORA), 0.01)
    w_uq = nrm(ks[9], (DEPTH, MLA_Q_LORA, MLA_HEADS, MLA_NOPE + MLA_ROPE), MLA_Q_LORA ** -0.5)
    kv_norm_w = 1.0 + nrm(ks[10], (DEPTH, MLA_KV_LORA), 0.01)
    w_uk = nrm(ks[11], (DEPTH, MLA_KV_LORA, MLA_HEADS, MLA_NOPE), MLA_KV_LORA ** -0.5)
    w_uv = nrm(ks[12], (DEPTH, MLA_KV_LORA, MLA_HEADS, MLA_V), MLA_KV_LORA ** -0.5)
    w_br_dn = nrm(ks[13], (DEPTH, DN_V_W, D_MODEL), DN_V_W ** -0.5)
    w_br_mla = nrm(ks[14], (DEPTH, MLA_V_W, D_MODEL), MLA_V_W ** -0.5)
    w_o = nrm(ks[15], (DEPTH, D_MODEL, D_MODEL), D_MODEL ** -0.5 * DEEPNORM_BETA)
    ln1_g = 1.0 + nrm(ks[16], (DEPTH, D_MODEL), 0.01)
    ln1_b = nrm(ks[17], (DEPTH, D_MODEL), 0.01)
    w_ffn_in = nrm(ks[18], (DEPTH, D_MODEL, 2 * FFN_HIDDEN), D_MODEL ** -0.5)
    w_ffn_out = nrm(ks[19], (DEPTH, FFN_HIDDEN, D_MODEL), FFN_HIDDEN ** -0.5 * DEEPNORM_BETA)
    w_ple = nrm(ks[20], (DEPTH, PLE_DIM, D_MODEL), PLE_DIM ** -0.5 * DEEPNORM_BETA)
    w_ple_gate = nrm(ks[21], (DEPTH, D_MODEL, D_MODEL), D_MODEL ** -0.5)
    ln2_g = 1.0 + nrm(ks[22], (DEPTH, D_MODEL), 0.01)
    ln2_b = nrm(ks[23], (DEPTH, D_MODEL), 0.01)
    return {'x': x, 'p': p, 'positions': positions, 'w_in': w_in, 'conv_w': conv_w, 'dn_a_log': dn_a_log, 'dn_dt_bias': dn_dt_bias, 'dn_norm_w': dn_norm_w, 'q_norm_w': q_norm_w, 'w_uq': w_uq, 'kv_norm_w': kv_norm_w, 'w_uk': w_uk, 'w_uv': w_uv, 'w_br_dn': w_br_dn, 'w_br_mla': w_br_mla, 'w_o': w_o, 'ln1_g': ln1_g, 'ln1_b': ln1_b, 'w_ffn_in': w_ffn_in, 'w_ffn_out': w_ffn_out, 'w_ple': w_ple, 'w_ple_gate': w_ple_gate, 'ln2_g': ln2_g, 'ln2_b': ln2_b}


def reference(x, p, positions, w_in, conv_w, dn_a_log, dn_dt_bias, dn_norm_w, q_norm_w, w_uq, kv_norm_w, w_uk, w_uv, w_br_dn, w_br_mla, w_o, ln1_g, ln1_b, w_ffn_in, w_ffn_out, w_ple, w_ple_gate, ln2_g, ln2_b):
    b, s, _ = x.shape
    cos, sin = rope_tables(positions)
    h = x
    for i in range(DEPTH):
        proj = h @ w_in[i]
        qkv, z, b_raw, a_raw, cq, ckv, kr, gate_dn, gate_mla = jnp.split(proj, SPLIT_IDX, axis=-1)

        qkv = jax.nn.silu(causal_depthwise_conv(qkv, conv_w[i]))
        dq, dk, dv = jnp.split(qkv, [DN_QK_W, 2 * DN_QK_W], axis=-1)
        dq = l2_normalize(dq.reshape(b, s, DN_HEADS, DN_DK)) * (DN_DK ** -0.5)
        dk = l2_normalize(dk.reshape(b, s, DN_HEADS, DN_DK))
        dv = dv.reshape(b, s, DN_HEADS, DN_DV).astype(jnp.float32)
        beta = jax.nn.sigmoid(b_raw.astype(jnp.float32))
        g = -jnp.exp(dn_a_log[i].astype(jnp.float32)) * jax.nn.softplus(a_raw.astype(jnp.float32) + dn_dt_bias[i].astype(jnp.float32))
        o_dn = gated_delta_rule(dq, dk, dv, beta, g)
        o_dn = rms_norm(o_dn, dn_norm_w[i]) * jax.nn.silu(z.reshape(b, s, DN_HEADS, DN_DV).astype(jnp.float32))
        y_dn = o_dn.reshape(b, s, DN_V_W).astype(h.dtype) @ w_br_dn[i]

        c_q = rms_norm(cq, q_norm_w[i]).astype(h.dtype)
        q_full = jnp.einsum('bsc,chd->bshd', c_q, w_uq[i])
        q_nope, q_rope = jnp.split(q_full, [MLA_NOPE], axis=-1)
        q_rope = apply_rope(q_rope, cos[:, :, None, :], sin[:, :, None, :])
        c_kv = rms_norm(ckv, kv_norm_w[i])
        k_rope = apply_rope(kr, cos, sin)
        q_lat = jnp.einsum('bshd,chd->bshc', q_nope.astype(jnp.float32), w_uk[i].astype(jnp.float32))
        out_lat = mla_attention(q_lat, q_rope, c_kv, k_rope)
        o_mla = jnp.einsum('bshc,chd->bshd', out_lat, w_uv[i].astype(jnp.float32))
        y_mla = o_mla.reshape(b, s, MLA_V_W).astype(h.dtype) @ w_br_mla[i]

        mixed = jax.nn.sigmoid(gate_dn) * y_dn + jax.nn.sigmoid(gate_mla) * y_mla
        h = layer_norm(DEEPNORM_ALPHA * h + mixed @ w_o[i], ln1_g[i], ln1_b[i])

        gt, up = jnp.split(h @ w_ffn_in[i], 2, axis=-1)
        ffn = (jax.nn.silu(gt) * up) @ w_ffn_out[i]
        ple = jax.nn.sigmoid(h @ w_ple_gate[i]) * (p[i] @ w_ple[i])
        h = layer_norm(DEEPNORM_ALPHA * h + ffn + ple, ln2_g[i], ln2_b[i])
    return h
```

```python
import functools
import math

import jax
import jax.numpy as jnp
from jax import lax
from jax.experimental import pallas as pl
from jax.experimental.pallas import tpu as pltpu

D_MODEL = 1024
PLE_DIM = 256
DN_HEADS = 8
DN_DK = 128
DN_DV = 128
DN_CONV = 4
DN_CHUNK = 64
DN_QK_W = DN_HEADS * DN_DK
DN_V_W = DN_HEADS * DN_DV
DN_CONV_CH = 2 * DN_QK_W + DN_V_W
MLA_HEADS = 8
MLA_Q_LORA = 384
MLA_KV_LORA = 256
MLA_NOPE = 128
MLA_ROPE = 64
MLA_V = 128
ROPE_BASE = 10000.0
FFN_HIDDEN = -(-8 * D_MODEL // (3 * 256)) * 256
DEPTH = 1
DEEPNORM_ALPHA = (2.0 * DEPTH) ** 0.25
IN_SIZES = (DN_CONV_CH, DN_V_W, DN_HEADS, DN_HEADS, MLA_Q_LORA, MLA_KV_LORA, MLA_ROPE, D_MODEL, D_MODEL)

LANES = 128
SUBLANES = 8
VMEM_LIMIT_BYTES = 56 * 1024 * 1024

LAT_W = 896
LAT_CKV = MLA_Q_LORA
LAT_KR = LAT_CKV + MLA_KV_LORA
LAT_BA = LAT_KR + 2 * MLA_ROPE

BF16 = jnp.bfloat16
F32 = jnp.float32
HIGHEST = lax.Precision.HIGHEST


def _dot(a, b):
    return jnp.dot(a, b, preferred_element_type=F32)


def _dot_nt(a, b):
    return lax.dot_general(a, b, (((1,), (1,)), ((), ())), preferred_element_type=F32)


def _dot_tn(a, b):
    return lax.dot_general(a, b, (((0,), (0,)), ((), ())), preferred_element_type=F32)


def _dot_hi(a, b):
    return jnp.dot(a, b, preferred_element_type=F32, precision=HIGHEST)


def _dot_nt_hi(a, b):
    return lax.dot_general(a, b, (((1,), (1,)), ((), ())), preferred_element_type=F32, precision=HIGHEST)


def _sigmoid(x):
    return 1.0 / (1.0 + jnp.exp(-x))


def _layer_norm(r, g, b):
    mu = jnp.mean(r, axis=-1, keepdims=True)
    c = r - mu
    var = jnp.mean(c * c, axis=-1, keepdims=True)
    return c * lax.rsqrt(var + 1e-5) * g + b


def _const_spec(shape):
    return pl.BlockSpec(shape, lambda *_: (0,) * len(shape))


def _params(n_axes, sem=None):
    return pltpu.CompilerParams(
        dimension_semantics=sem or ("arbitrary",) * n_axes, vmem_limit_bytes=VMEM_LIMIT_BYTES)


QKV_TM = 512
QKV_SLAB = 512


def _dn_qkv_kernel(tiles_per_seq, x_ref, w_ref, cw_ref, q_ref, k_ref, v_ref, carry_ref):
    i = pl.program_id(0)
    tm = x_ref.shape[0]

    @pl.when(i % tiles_per_seq == 0)
    def _():
        carry_ref[...] = jnp.zeros_like(carry_ref)

    xb = x_ref[...].astype(BF16)
    outs = (q_ref, k_ref, v_ref)
    slabs_per_out = DN_QK_W // QKV_SLAB
    for j in range(DN_CONV_CH // QKV_SLAB):
        cs = slice(j * QKV_SLAB, (j + 1) * QKV_SLAB)
        pre = _dot(xb, w_ref[:, cs])
        ext = jnp.concatenate([carry_ref[:, cs], pre], axis=0)
        cw = cw_ref[:, cs]
        y = pre * cw[3:4]
        for tap in range(DN_CONV - 1):
            shift = DN_CONV - 1 - tap
            y = y + ext[SUBLANES - shift:SUBLANES - shift + tm] * cw[tap:tap + 1]
        carry_ref[:, cs] = pre[tm - SUBLANES:]
        y = y * _sigmoid(y)
        out_ref = outs[j // slabs_per_out]
        col0 = (j % slabs_per_out) * QKV_SLAB
        which = j // slabs_per_out
        for hh in range(QKV_SLAB // DN_DK):
            yh = y[:, hh * DN_DK:(hh + 1) * DN_DK]
            if which < 2:
                ss = jnp.sum(yh * yh, axis=-1, keepdims=True)
                yh = yh * lax.rsqrt(ss + 1e-6)
                if which == 0:
                    yh = yh * (DN_DK ** -0.5)
            out_ref[:, col0 + hh * DN_DK:col0 + (hh + 1) * DN_DK] = yh.astype(out_ref.dtype)


def _dn_qkv(x2, w_qkv, conv_w, seq):
    t = x2.shape[0]
    tm = QKV_TM
    out = jax.ShapeDtypeStruct((t, DN_QK_W), F32)
    row = lambda i: (i, 0)
    return pl.pallas_call(
        functools.partial(_dn_qkv_kernel, seq // tm),
        grid=(t // tm,),
        in_specs=[pl.BlockSpec((tm, D_MODEL), row), _const_spec((D_MODEL, DN_CONV_CH)),
                  _const_spec((DN_CONV, DN_CONV_CH))],
        out_specs=[pl.BlockSpec((tm, DN_QK_W), row)] * 3,
        out_shape=[out] * 3,
        scratch_shapes=[pltpu.VMEM((SUBLANES, DN_CONV_CH), F32)],
        compiler_params=_params(1),
        name="dn_qkv",
    )(x2, w_qkv, conv_w)


LAT_TM = 512
MLA_SCALE = (MLA_NOPE + MLA_ROPE) ** -0.5


def _rms(t, w):
    return t * lax.rsqrt(jnp.mean(t * t, axis=-1, keepdims=True) + 1e-6) * w


def _latent_kernel(x_ref, wl_ref, pos_ref, invf_ref, qnw_ref, kvw_ref, wqn_ref, wqr_ref, wuk_ref, wuv_ref,
                   dnp_ref, qn_ref, qr_ref, kn_ref, kr_ref, v_ref, bg_ref):
    xb = x_ref[...].astype(BF16)
    lat = _dot(xb, wl_ref[...])

    lane = lax.broadcasted_iota(jnp.int32, (1, LANES), 1)
    ang = pos_ref[...].astype(F32) * invf_ref[...]
    cos_t = jnp.where(lane < MLA_ROPE, jnp.cos(ang), 0.0)
    sin_a = jnp.sin(ang)
    half = MLA_ROPE // 2
    sin_t = jnp.where(lane < half, -sin_a, jnp.where(lane < MLA_ROPE, sin_a, 0.0))

    def rope(a):
        return a * cos_t + pltpu.roll(a, MLA_ROPE, axis=1) * sin_t

    c_q = _rms(lat[:, :MLA_Q_LORA], qnw_ref[...]).astype(BF16)
    qn_ref[...] = (_dot(c_q, wqn_ref[...]) * MLA_SCALE).astype(qn_ref.dtype)
    qr_raw = _dot(c_q, wqr_ref[...])
    for h in range(MLA_HEADS):
        hs = slice(h * LANES, (h + 1) * LANES)
        qr_ref[:, hs] = (rope(qr_raw[:, hs]) * MLA_SCALE).astype(qr_ref.dtype)

    c_kv = _rms(lat[:, LAT_CKV:LAT_KR], kvw_ref[...]).astype(BF16)
    kn_ref[...] = _dot(c_kv, wuk_ref[...]).astype(kn_ref.dtype)
    v_ref[...] = _dot(c_kv, wuv_ref[...]).astype(v_ref.dtype)
    kr_ref[...] = rope(lat[:, LAT_KR:LAT_BA]).astype(kr_ref.dtype)

    ba = lat[:, LAT_BA:LAT_BA + LANES]
    beta = _sigmoid(ba)
    sp_in = ba + dnp_ref[1:2, :]
    softplus = jnp.maximum(sp_in, 0.0) + jnp.log(1.0 + jnp.exp(-jnp.abs(sp_in)))
    g = -jnp.exp(dnp_ref[0:1, :]) * softplus
    bg_ref[...] = jnp.where(lane < DN_HEADS, beta, jnp.where(lane < 2 * DN_HEADS, g, 0.0))


def _latent(x2, w_lat, pos2, invf, qnw, kvw, wqn, wqr, wuk, wuv, dnp):
    t = x2.shape[0]
    tm = LAT_TM
    row = lambda i: (i, 0)
    wide = jax.ShapeDtypeStruct((t, MLA_HEADS * LANES), BF16)
    return pl.pallas_call(
        _latent_kernel,
        grid=(t // tm,),
        in_specs=[pl.BlockSpec((tm, D_MODEL), row), _const_spec(w_lat.shape), pl.BlockSpec((tm, 1), row),
                  _const_spec(invf.shape), _const_spec(qnw.shape), _const_spec(kvw.shape),
                  _const_spec(wqn.shape), _const_spec(wqr.shape), _const_spec(wuk.shape), _const_spec(wuv.shape),
                  _const_spec(dnp.shape)],
        out_specs=[pl.BlockSpec((tm, MLA_HEADS * LANES), row)] * 3 + [pl.BlockSpec((tm, LANES), row)]
        + [pl.BlockSpec((tm, MLA_HEADS * LANES), row), pl.BlockSpec((tm, LANES), row)],
        out_shape=[wide, wide, wide, jax.ShapeDtypeStruct((t, LANES), BF16), wide,
                   jax.ShapeDtypeStruct((t, LANES), F32)],
        compiler_params=_params(1),
        name="latent",
    )(x2, w_lat, pos2, invf, qnw, kvw, wqn, wqr, wuk, wuv, dnp)


PREP_ROWS = 512
C = DN_CHUNK


def _tri_inverse(l_mat, row, col):
    eye = (row == col).astype(F32)
    inv = eye
    s = 1
    while s < C:
        blk = ((row // (2 * s)) == (col // (2 * s))) & ((row % (2 * s)) >= s) & ((col % (2 * s)) < s)
        off = jnp.where(blk, l_mat, 0.0)
        if s == 1:
            inv = inv - off
        else:
            inv = inv - _dot_hi(inv, _dot_hi(off, inv))
        s *= 2
    return inv


def _dn_prep_kernel(q_ref, k_ref, v_ref, bg_ref, w_ref, qd_ref, kt_ref, u_ref, intra_ref, egl_ref):
    h = pl.program_id(1)
    rows = q_ref.shape[0]
    lane_r = lax.broadcasted_iota(jnp.int32, (LANES, LANES), 0)
    sel_b = (lane_r == h).astype(F32)
    sel_g = (lane_r == h + DN_HEADS).astype(F32)
    bg = bg_ref[...]
    beta_b = _dot_hi(bg, sel_b)
    g_b = _dot_hi(bg, sel_g)

    row = lax.broadcasted_iota(jnp.int32, (C, C), 0)
    col = lax.broadcasted_iota(jnp.int32, (C, C), 1)
    tril = row >= col
    strict = row > col
    tril_f = tril.astype(F32)
    first_lane = (lax.broadcasted_iota(jnp.int32, (C, LANES), 1) == 0).astype(F32)

    for c in range(rows // C):
        rs = slice(c * C, (c + 1) * C)
        q = q_ref[rs, :]
        k = k_ref[rs, :]
        v = v_ref[rs, :]
        beta = beta_b[rs, :]
        gc = _dot_hi(tril_f, g_b[rs, :])
        egc = jnp.exp(gc)
        gl = gc[C - 1:C, :]
        g_rowform = _dot_nt_hi(first_lane, gc)
        diff = gc[:, :C] - g_rowform
        decay = jnp.where(tril, jnp.exp(jnp.where(tril, diff, 0.0)), 0.0)
        kb = k * beta
        kbf = k.astype(BF16)
        l_mat = jnp.where(strict, _dot_nt(kb.astype(BF16), kbf) * decay, 0.0)
        t_inv = _tri_inverse(l_mat, row, col).astype(BF16)
        u_ref[rs, :] = _dot(t_inv, (v * beta).astype(BF16))
        w_ref[rs, :] = _dot(t_inv, (kb * egc).astype(BF16)).astype(w_ref.dtype)
        intra = jnp.where(tril, _dot_nt(q.astype(BF16), kbf) * decay, 0.0)
        intra_ref[0, rs, :] = intra.astype(intra_ref.dtype)
        qd_ref[rs, :] = (q * egc).astype(qd_ref.dtype)
        kt_ref[rs, :] = (k * jnp.exp(gl - gc)).astype(kt_ref.dtype)
        egl_ref[0, c:c + 1, :] = jnp.exp(gl)


def _dn_prep(q, k, v, bg):
    t = q.shape[0]
    rows = PREP_ROWS
    hb = lambda i, h: (i, h)
    blk = pl.BlockSpec((rows, DN_DK), hb)
    wide_bf = jax.ShapeDtypeStruct((t, DN_QK_W), BF16)
    return pl.pallas_call(
        _dn_prep_kernel,
        grid=(t // rows, DN_HEADS),
        in_specs=[blk, blk, blk, pl.BlockSpec((rows, LANES), lambda i, h: (i, 0))],
        out_specs=[blk, blk, blk, blk, pl.BlockSpec((1, rows, C), lambda i, h: (h, i, 0)),
                   pl.BlockSpec((1, rows // C, LANES), lambda i, h: (h, i, 0))],
        out_shape=[wide_bf, wide_bf, wide_bf, jax.ShapeDtypeStruct((t, DN_V_W), F32),
                   jax.ShapeDtypeStruct((DN_HEADS, t, C), BF16),
                   jax.ShapeDtypeStruct((DN_HEADS, t // C, LANES), F32)],
        compiler_params=_params(2),
        name="dn_prep",
    )(q, k, v, bg)


SCAN_ROWS = 512


def _dn_scan_kernel(w_ref, qd_ref, kt_ref, u_ref, intra_ref, egl_ref, o_ref, state_ref):
    @pl.when(pl.program_id(1) == 0)
    def _():
        state_ref[...] = jnp.zeros_like(state_ref)

    rows = w_ref.shape[0]
    for c in range(rows // C):
        rs = slice(c * C, (c + 1) * C)
        for h in range(DN_HEADS):
            hs = slice(h * DN_DK, (h + 1) * DN_DK)
            state = state_ref[h]
            wq = jnp.concatenate([w_ref[rs, hs], qd_ref[rs, hs]], axis=0)
            r1 = _dot(wq, state.astype(BF16))
            v_new = u_ref[rs, hs] - r1[:C]
            vb = v_new.astype(BF16)
            o_ref[rs, hs] = r1[C:] + _dot(intra_ref[h, rs, :], vb)
            state_ref[h] = state * egl_ref[h, c:c + 1, :] + _dot_tn(kt_ref[rs, hs], vb)


def _dn_scan(w, qd, kt, u, intra, egl, batch, seq):
    t = w.shape[0]
    rows = SCAN_ROWS
    steps = seq // rows
    row = lambda b, i: (b * steps + i, 0)
    wide = pl.BlockSpec((rows, DN_QK_W), row)
    return pl.pallas_call(
        _dn_scan_kernel,
        grid=(batch, steps),
        in_specs=[wide, wide, wide, wide,
                  pl.BlockSpec((DN_HEADS, rows, C), lambda b, i: (0, b * steps + i, 0)),
                  pl.BlockSpec((DN_HEADS, rows // C, LANES), lambda b, i: (0, b * steps + i, 0))],
        out_specs=wide,
        out_shape=jax.ShapeDtypeStruct((t, DN_V_W), F32),
        scratch_shapes=[pltpu.VMEM((DN_HEADS, DN_DK, DN_DV), F32)],
        compiler_params=_params(2),
        name="dn_scan",
    )(w, qd, kt, u, intra, egl)


ATT_TQ = 512
ATT_TK = 512
NEG_INF = -1e30


def _mla_attn_kernel(qn_ref, qr_ref, kn_ref, kr_ref, v_ref, o_ref, kcat_ref, m_ref, l_ref, acc_ref):
    i = pl.program_id(2)
    tq = qn_ref.shape[0]
    tk = ATT_TK

    @pl.when(i == 0)
    def _():
        kcat_ref[:, :LANES] = kn_ref[...]
        kcat_ref[:, LANES:] = kr_ref[...]

    q = jnp.concatenate([qn_ref[...], qr_ref[...]], axis=1)
    m_ref[...] = jnp.full_like(m_ref, NEG_INF)
    l_ref[...] = jnp.zeros_like(l_ref)
    acc_ref[...] = jnp.zeros_like(acc_ref)

    def step(j, masked):
        start = pl.multiple_of(j * tk, tk)
        s = _dot_nt(q, kcat_ref[pl.ds(start, tk), :])
        if masked:
            qi = lax.broadcasted_iota(jnp.int32, (tq, tk), 0)
            ki = lax.broadcasted_iota(jnp.int32, (tq, tk), 1)
            s = jnp.where(ki <= qi, s, NEG_INF)
        m_prev = m_ref[...]
        m_new = jnp.maximum(m_prev, jnp.max(s, axis=1, keepdims=True))
        alpha = jnp.exp(m_prev - m_new)
        p = jnp.exp(s - pltpu.repeat(m_new, tk // LANES, 1))
        l_ref[...] = alpha * l_ref[...] + jnp.sum(p, axis=1, keepdims=True)
        acc_ref[...] = alpha * acc_ref[...] + _dot(p.astype(BF16), v_ref[pl.ds(start, tk), :])
        m_ref[...] = m_new

    def body(j, carry):
        step(j, False)
        return carry

    lax.fori_loop(0, i, body, 0)
    step(i, True)
    o_ref[...] = (acc_ref[...] / l_ref[...]).astype(o_ref.dtype)


def _mla_attn(qn, qr, kn, kr, v, batch, seq):
    t = qn.shape[0]
    tq = ATT_TQ
    nq = seq // tq
    qblk = pl.BlockSpec((tq, LANES), lambda b, h, i: (b * nq + i, h))
    kblk = pl.BlockSpec((seq, LANES), lambda b, h, i: (b, h))
    return pl.pallas_call(
        _mla_attn_kernel,
        grid=(batch, MLA_HEADS, nq),
        in_specs=[qblk, qblk, kblk, pl.BlockSpec((seq, LANES), lambda b, h, i: (b, 0)), kblk],
        out_specs=qblk,
        out_shape=jax.ShapeDtypeStruct((t, MLA_HEADS * MLA_V), BF16),
        scratch_shapes=[pltpu.VMEM((seq, 2 * LANES), BF16), pltpu.VMEM((tq, LANES), F32),
                        pltpu.VMEM((tq, LANES), F32), pltpu.VMEM((tq, LANES), F32)],
        compiler_params=_params(3),
        name="mla_attn",
    )(qn, qr, kn, kr, v)


MERGE_TM = 256


def _merge_kernel(x_ref, odn_ref, omla_ref, wzg_ref, wdn_ref, wmla_ref, wo_ref, nw_ref, g_ref, b_ref, h_ref):
    x = x_ref[...]
    zg = _dot(x.astype(BF16), wzg_ref[...])
    z = zg[:, :DN_V_W]
    parts = []
    for h in range(DN_HEADS):
        hs = slice(h * DN_DV, (h + 1) * DN_DV)
        parts.append(_rms(odn_ref[:, hs], nw_ref[...]))
    o_dn = jnp.concatenate(parts, axis=1) * (z * _sigmoid(z))
    y_dn = _dot(o_dn.astype(BF16), wdn_ref[...])
    y_mla = _dot(omla_ref[...], wmla_ref[...])
    mixed = _sigmoid(zg[:, DN_V_W:DN_V_W + D_MODEL]) * y_dn + _sigmoid(zg[:, DN_V_W + D_MODEL:]) * y_mla
    r = DEEPNORM_ALPHA * x + _dot(mixed.astype(BF16), wo_ref[...])
    h_ref[...] = _layer_norm(r, g_ref[...], b_ref[...])


def _merge(x2, o_dn, o_mla, wzg, wdn, wmla, wo, nw, g, b):
    t = x2.shape[0]
    tm = MERGE_TM
    row = lambda i: (i, 0)
    act = pl.BlockSpec((tm, D_MODEL), row)
    return pl.pallas_call(
        _merge_kernel,
        grid=(t // tm,),
        in_specs=[act, act, act, _const_spec(wzg.shape), _const_spec(wdn.shape), _const_spec(wmla.shape),
                  _const_spec(wo.shape), _const_spec(nw.shape), _const_spec(g.shape), _const_spec(b.shape)],
        out_specs=act,
        out_shape=jax.ShapeDtypeStruct((t, D_MODEL), F32),
        compiler_params=_params(1),
        name="merge",
    )(x2, o_dn, o_mla, wzg, wdn, wmla, wo, nw, g, b)


FFN_TM = 512
FFN_SLAB = 256


def _ffn_kernel(h_ref, p_ref, wg_ref, wu_ref, wout_ref, wpg_ref, wple_ref, g_ref, b_ref, o_ref):
    h = h_ref[...]
    hb = h.astype(BF16)
    acc = jnp.zeros(h.shape, F32)
    for j in range(FFN_HIDDEN // FFN_SLAB):
        cs = slice(j * FFN_SLAB, (j + 1) * FFN_SLAB)
        gt = _dot(hb, wg_ref[:, cs])
        up = _dot(hb, wu_ref[:, cs])
        a = (gt * _sigmoid(gt) * up).astype(BF16)
        acc = acc + _dot(a, wout_ref[cs, :])
    ple = _sigmoid(_dot(hb, wpg_ref[...])) * _dot(p_ref[...].astype(BF16), wple_ref[...])
    r = DEEPNORM_ALPHA * h + acc + ple
    o_ref[...] = _layer_norm(r, g_ref[...], b_ref[...])


def _ffn(h1, p2, wg, wu, wout, wpg, wple, g, b):
    t = h1.shape[0]
    tm = FFN_TM
    row = lambda i: (i, 0)
    act = pl.BlockSpec((tm, D_MODEL), row)
    return pl.pallas_call(
        _ffn_kernel,
        grid=(t // tm,),
        in_specs=[act, pl.BlockSpec((tm, PLE_DIM), row), _const_spec(wg.shape), _const_spec(wu.shape),
                  _const_spec(wout.shape), _const_spec(wpg.shape), _const_spec(wple.shape),
                  _const_spec(g.shape), _const_spec(b.shape)],
        out_specs=act,
        out_shape=jax.ShapeDtypeStruct((t, D_MODEL), F32),
        compiler_params=_params(1),
        name="ffn",
    )(h1, p2, wg, wu, wout, wpg, wple, g, b)


def _split_w_in(w_in):
    offs = [0]
    for n in IN_SIZES:
        offs.append(offs[-1] + n)
    return [w_in[:, offs[i]:offs[i + 1]] for i in range(len(IN_SIZES))]


def kernel(x, p, positions, w_in, conv_w, dn_a_log, dn_dt_bias, dn_norm_w, q_norm_w, w_uq, kv_norm_w, w_uk, w_uv,
           w_br_dn, w_br_mla, w_o, ln1_g, ln1_b, w_ffn_in, w_ffn_out, w_ple, w_ple_gate, ln2_g, ln2_b):
    assert w_in.shape[0] == DEPTH
    batch, seq, _ = x.shape
    t = batch * seq
    h = x.reshape(t, D_MODEL)
    pos2 = positions.reshape(t, 1)
    half = MLA_ROPE // 2
    invf = ROPE_BASE ** (-jnp.arange(0, MLA_ROPE, 2, dtype=F32) / MLA_ROPE)
    invf = jnp.tile(invf, LANES // half).reshape(1, LANES)

    for i in range(DEPTH):
        w_qkv, w_z, w_b, w_a, w_cq, w_ckv, w_kr, w_gdn, w_gmla = _split_w_in(w_in[i])
        w_kr_sw = jnp.concatenate([w_kr[:, half:], w_kr[:, :half]], axis=1)
        pad = jnp.zeros((D_MODEL, LAT_W - LAT_BA - 2 * DN_HEADS), F32)
        w_lat = jnp.concatenate([w_cq, w_ckv, w_kr, w_kr_sw, w_b, w_a, pad], axis=1).astype(BF16)
        w_zg = jnp.concatenate([w_z, w_gdn, w_gmla], axis=1).astype(BF16)
        uq = w_uq[i]
        t1 = uq[:, :, MLA_NOPE:MLA_NOPE + half]
        t2 = uq[:, :, MLA_NOPE + half:]
        wqn = uq[:, :, :MLA_NOPE].reshape(MLA_Q_LORA, MLA_HEADS * MLA_NOPE).astype(BF16)
        wqr = jnp.concatenate([t1, t2, t2, t1], axis=2).reshape(MLA_Q_LORA, MLA_HEADS * LANES).astype(BF16)
        wuk = w_uk[i].reshape(MLA_KV_LORA, MLA_HEADS * MLA_NOPE).astype(BF16)
        wuv = w_uv[i].reshape(MLA_KV_LORA, MLA_HEADS * MLA_V).astype(BF16)
        lane_pad = jnp.zeros((LANES - 2 * DN_HEADS,), F32)
        head_pad = jnp.zeros((DN_HEADS,), F32)
        dnp = jnp.stack([jnp.concatenate([head_pad, dn_a_log[i], lane_pad]),
                         jnp.concatenate([head_pad, dn_dt_bias[i], lane_pad])])
        nw = jnp.tile(dn_norm_w[i], 1).reshape(1, DN_DV)
        w_ffn = w_ffn_in[i].astype(BF16)

        q, k, v = _dn_qkv(h, w_qkv.astype(BF16), conv_w[i], seq)
        qn, qr, kn, kr, vm, bg = _latent(h, w_lat, pos2, invf, q_norm_w[i].reshape(1, -1),
                                         kv_norm_w[i].reshape(1, -1), wqn, wqr, wuk, wuv, dnp)
        w, qd, kt, u, intra, egl = _dn_prep(q, k, v, bg)
        o_dn = _dn_scan(w, qd, kt, u, intra, egl, batch, seq)
        o_mla = _mla_attn(qn, qr, kn, kr, vm, batch, seq)
        h1 = _merge(h, o_dn, o_mla, w_zg, w_br_dn[i].astype(BF16), w_br_mla[i].astype(BF16),
                    w_o[i].astype(BF16), nw, ln1_g[i].reshape(1, -1), ln1_b[i].reshape(1, -1))
        h = _ffn(h1, p[i].reshape(t, PLE_DIM), w_ffn[:, :FFN_HIDDEN], w_ffn[:, FFN_HIDDEN:],
                 w_ffn_out[i].astype(BF16), w_ple_gate[i].astype(BF16), w_ple[i].astype(BF16),
                 ln2_g[i].reshape(1, -1), ln2_b[i].reshape(1, -1))
    return h.reshape(batch, seq, D_MODEL)
```

```python
import functools
import math

import jax
import jax.numpy as jnp
from jax import lax
from jax.experimental import pallas as pl
from jax.experimental.pallas import tpu as pltpu

D_MODEL = 1024
PLE_DIM = 256
DN_HEADS = 8
DN_DK = 128
DN_DV = 128
DN_CONV = 4
DN_CHUNK = 64
DN_QK_W = DN_HEADS * DN_DK
DN_V_W = DN_HEADS * DN_DV
DN_CONV_CH = 2 * DN_QK_W + DN_V_W
MLA_HEADS = 8
MLA_Q_LORA = 384
MLA_KV_LORA = 256
MLA_NOPE = 128
MLA_ROPE = 64
MLA_V = 128
ROPE_BASE = 10000.0
FFN_HIDDEN = -(-8 * D_MODEL // (3 * 256)) * 256
DEPTH = 1
DEEPNORM_ALPHA = (2.0 * DEPTH) ** 0.25
IN_SIZES = (DN_CONV_CH, DN_V_W, DN_HEADS, DN_HEADS, MLA_Q_LORA, MLA_KV_LORA, MLA_ROPE, D_MODEL, D_MODEL)

LANES = 128
SUBLANES = 8
VMEM_LIMIT_BYTES = 56 * 1024 * 1024

LAT_W = 896
LAT_CKV = MLA_Q_LORA
LAT_KR = LAT_CKV + MLA_KV_LORA
LAT_BA = LAT_KR + 2 * MLA_ROPE

BF16 = jnp.bfloat16
F32 = jnp.float32


def _dot(a, b):
    return jnp.dot(a, b, preferred_element_type=F32)


def _dot_nt(a, b):
    return lax.dot_general(a, b, (((1,), (1,)), ((), ())), preferred_element_type=F32)


def _dot_tn(a, b):
    return lax.dot_general(a, b, (((0,), (0,)), ((), ())), preferred_element_type=F32)


def _sigmoid(x):
    return 1.0 / (1.0 + jnp.exp(-x))


def _layer_norm(r, g, b):
    mu = jnp.mean(r, axis=-1, keepdims=True)
    c = r - mu
    var = jnp.mean(c * c, axis=-1, keepdims=True)
    return c * lax.rsqrt(var + 1e-5) * g + b


def _const_spec(shape):
    return pl.BlockSpec(shape, lambda *_: (0,) * len(shape))


def _params(n_axes, sem=None):
    return pltpu.CompilerParams(
        dimension_semantics=sem or ("arbitrary",) * n_axes, vmem_limit_bytes=VMEM_LIMIT_BYTES)


QKV_TM = 512
QKV_SLAB = 512


def _dn_qkv_kernel(tiles_per_seq, x_ref, w_ref, cw_ref, q_ref, k_ref, v_ref, carry_ref):
    i = pl.program_id(0)
    tm = x_ref.shape[0]

    @pl.when(i % tiles_per_seq == 0)
    def _():
        carry_ref[...] = jnp.zeros_like(carry_ref)

    xb = x_ref[...].astype(BF16)
    outs = (q_ref, k_ref, v_ref)
    slabs_per_out = DN_QK_W // QKV_SLAB
    for j in range(DN_CONV_CH // QKV_SLAB):
        cs = slice(j * QKV_SLAB, (j + 1) * QKV_SLAB)
        pre = _dot(xb, w_ref[:, cs])
        ext = jnp.concatenate([carry_ref[:, cs], pre], axis=0)
        cw = cw_ref[:, cs]
        y = pre * cw[3:4]
        for tap in range(DN_CONV - 1):
            shift = DN_CONV - 1 - tap
            y = y + ext[SUBLANES - shift:SUBLANES - shift + tm] * cw[tap:tap + 1]
        carry_ref[:, cs] = pre[tm - SUBLANES:]
        y = y * _sigmoid(y)
        out_ref = outs[j // slabs_per_out]
        col0 = (j % slabs_per_out) * QKV_SLAB
        which = j // slabs_per_out
        for hh in range(QKV_SLAB // DN_DK):
            yh = y[:, hh * DN_DK:(hh + 1) * DN_DK]
            if which < 2:
                ss = jnp.sum(yh * yh, axis=-1, keepdims=True)
                yh = yh * lax.rsqrt(ss + 1e-6)
                if which == 0:
                    yh = yh * (DN_DK ** -0.5)
            out_ref[:, col0 + hh * DN_DK:col0 + (hh + 1) * DN_DK] = yh.astype(out_ref.dtype)


def _dn_qkv(x2, w_qkv, conv_w, seq):
    t = x2.shape[0]
    tm = QKV_TM
    out = jax.ShapeDtypeStruct((t, DN_QK_W), F32)
    row = lambda i: (i, 0)
    return pl.pallas_call(
        functools.partial(_dn_qkv_kernel, seq // tm),
        grid=(t // tm,),
        in_specs=[pl.BlockSpec((tm, D_MODEL), row), _const_spec((D_MODEL, DN_CONV_CH)),
                  _const_spec((DN_CONV, DN_CONV_CH))],
        out_specs=[pl.BlockSpec((tm, DN_QK_W), row)] * 3,
        out_shape=[out] * 3,
        scratch_shapes=[pltpu.VMEM((SUBLANES, DN_CONV_CH), F32)],
        compiler_params=_params(1),
        name="dn_qkv",
    )(x2, w_qkv, conv_w)


LAT_TM = 512
MLA_SCALE = (MLA_NOPE + MLA_ROPE) ** -0.5 * math.log2(math.e)


def _rms(t, w):
    return t * lax.rsqrt(jnp.mean(t * t, axis=-1, keepdims=True) + 1e-6) * w


def _latent_kernel(x_ref, wl_ref, pos_ref, invf_ref, qnw_ref, kvw_ref, wqn_ref, wqr_ref, wuk_ref, wuv_ref,
                   dnp_ref, qn_ref, qr_ref, kn_ref, kr_ref, v_ref, bg_ref):
    xb = x_ref[...].astype(BF16)
    lat = _dot(xb, wl_ref[...])

    lane = lax.broadcasted_iota(jnp.int32, (1, LANES), 1)
    ang = pos_ref[...].astype(F32) * invf_ref[...]
    cos_t = jnp.where(lane < MLA_ROPE, jnp.cos(ang), 0.0)
    sin_a = jnp.sin(ang)
    half = MLA_ROPE // 2
    sin_t = jnp.where(lane < half, -sin_a, jnp.where(lane < MLA_ROPE, sin_a, 0.0))

    def rope(a):
        return a * cos_t + pltpu.roll(a, MLA_ROPE, axis=1) * sin_t

    c_q = _rms(lat[:, :MLA_Q_LORA], qnw_ref[...]).astype(BF16)
    qn_ref[...] = (_dot(c_q, wqn_ref[...]) * MLA_SCALE).astype(qn_ref.dtype)
    qr_raw = _dot(c_q, wqr_ref[...])
    for h in range(MLA_HEADS):
        hs = slice(h * LANES, (h + 1) * LANES)
        qr_ref[:, hs] = (rope(qr_raw[:, hs]) * MLA_SCALE).astype(qr_ref.dtype)

    c_kv = _rms(lat[:, LAT_CKV:LAT_KR], kvw_ref[...]).astype(BF16)
    kn_ref[...] = _dot(c_kv, wuk_ref[...]).astype(kn_ref.dtype)
    v_ref[...] = _dot(c_kv, wuv_ref[...]).astype(v_ref.dtype)
    kr_ref[...] = rope(lat[:, LAT_KR:LAT_BA]).astype(kr_ref.dtype)

    ba = lat[:, LAT_BA:LAT_BA + LANES]
    beta = _sigmoid(ba)
    sp_in = ba + dnp_ref[1:2, :]
    softplus = jnp.maximum(sp_in, 0.0) + jnp.log(1.0 + jnp.exp(-jnp.abs(sp_in)))
    g = -jnp.exp(dnp_ref[0:1, :]) * softplus
    bg_ref[...] = jnp.where(lane < DN_HEADS, beta, jnp.where(lane < 2 * DN_HEADS, g, 0.0))


def _latent(x2, w_lat, pos2, invf, qnw, kvw, wqn, wqr, wuk, wuv, dnp):
    t = x2.shape[0]
    tm = LAT_TM
    row = lambda i: (i, 0)
    wide = jax.ShapeDtypeStruct((t, MLA_HEADS * LANES), BF16)
    return pl.pallas_call(
        _latent_kernel,
        grid=(t // tm,),
        in_specs=[pl.BlockSpec((tm, D_MODEL), row), _const_spec(w_lat.shape), pl.BlockSpec((tm, 1), row),
                  _const_spec(invf.shape), _const_spec(qnw.shape), _const_spec(kvw.shape),
                  _const_spec(wqn.shape), _const_spec(wqr.shape), _const_spec(wuk.shape), _const_spec(wuv.shape),
                  _const_spec(dnp.shape)],
        out_specs=[pl.BlockSpec((tm, MLA_HEADS * LANES), row)] * 3 + [pl.BlockSpec((tm, LANES), row)]
        + [pl.BlockSpec((tm, MLA_HEADS * LANES), row), pl.BlockSpec((tm, LANES), row)],
        out_shape=[wide, wide, wide, jax.ShapeDtypeStruct((t, LANES), BF16), wide,
                   jax.ShapeDtypeStruct((t, LANES), F32)],
        compiler_params=_params(1),
        name="latent",
    )(x2, w_lat, pos2, invf, qnw, kvw, wqn, wqr, wuk, wuv, dnp)


PREP_ROWS = 512
C = DN_CHUNK


def _split3(a):
    a1 = a.astype(BF16)
    r1 = a - a1.astype(F32)
    a2 = r1.astype(BF16)
    a3 = (r1 - a2.astype(F32)).astype(BF16)
    return a1, a2, a3


def _split2(a):
    hi = a.astype(BF16)
    return hi, (a - hi.astype(F32)).astype(BF16)


def _mm_split(a, b):
    ah, al = _split2(a)
    bh, bl = _split2(b)
    return _dot(ah, bh) + _dot(ah, bl) + _dot(al, bh)


def _dn_prep_kernel(q_ref, k_ref, v_ref, bg_ref, w_ref, qd_ref, kt_ref, u_ref, intra_ref, egl_ref):
    h = pl.program_id(1)
    rows = q_ref.shape[0]
    nc = rows // C

    lane_r = lax.broadcasted_iota(jnp.int32, (3 * LANES, LANES), 0) % LANES
    sel_b = (lane_r == h).astype(BF16)
    sel_g = (lane_r == h + DN_HEADS).astype(BF16)
    bg3 = jnp.concatenate(_split3(bg_ref[...]), axis=1)
    beta_b = _dot(bg3, sel_b)
    g_b = _dot(bg3, sel_g)

    row = lax.broadcasted_iota(jnp.int32, (C, C), 0)
    col = lax.broadcasted_iota(jnp.int32, (C, C), 1)
    tril = row >= col
    strict = row > col
    eye = (row == col).astype(F32)
    tril3 = (lax.broadcasted_iota(jnp.int32, (C, 3 * C), 1) % C
             <= lax.broadcasted_iota(jnp.int32, (C, 3 * C), 0)).astype(BF16)
    lane_c = lax.broadcasted_iota(jnp.int32, (C, LANES), 1)
    pick3 = (lane_c < 3).astype(BF16)

    chunks = [slice(c * C, (c + 1) * C) for c in range(nc)]
    gcs = [_dot(tril3, jnp.concatenate(_split3(g_b[rs, :]), axis=0)) for rs in chunks]
    rowforms = []
    for gc in gcs:
        g1, g2, g3 = _split3(gc)
        mix = jnp.where(lane_c == 0, g1, jnp.where(lane_c == 1, g2, g3))
        rowforms.append(_dot_nt(pick3, mix))
    decays = [jnp.where(tril, jnp.exp(jnp.where(tril, gc[:, :C] - rf, 0.0)), 0.0)
              for gc, rf in zip(gcs, rowforms)]
    kbs = [k_ref[rs, :] * beta_b[rs, :] for rs in chunks]
    kq = [_dot_nt(jnp.concatenate([kb.astype(BF16), q_ref[rs, :].astype(BF16)], axis=0),
                  k_ref[rs, :].astype(BF16)) for kb, rs in zip(kbs, chunks)]
    l_mats = [jnp.where(strict, a[:C] * d, 0.0) for a, d in zip(kq, decays)]

    invs = [eye] * nc
    s = 1
    while s < C:
        blk = ((row // (2 * s)) == (col // (2 * s))) & ((row % (2 * s)) >= s) & ((col % (2 * s)) < s)
        offs = [jnp.where(blk, l, 0.0) for l in l_mats]
        if s == 1:
            invs = [inv - off for inv, off in zip(invs, offs)]
        else:
            mids = [_mm_split(off, inv) for off, inv in zip(offs, invs)]
            invs = [inv - _mm_split(inv, mid) for inv, mid in zip(invs, mids)]
        s *= 2

    egcs = [jnp.exp(gc) for gc in gcs]
    uws = [_dot(inv.astype(BF16),
                jnp.concatenate([(v_ref[rs, :] * beta_b[rs, :]).astype(BF16), (kb * egc).astype(BF16)], axis=1))
           for inv, rs, kb, egc in zip(invs, chunks, kbs, egcs)]
    for c, rs in enumerate(chunks):
        gc = gcs[c]
        gl = gc[C - 1:C, :]
        u_ref[rs, :] = uws[c][:, :DN_DV]
        w_ref[rs, :] = uws[c][:, DN_DV:].astype(w_ref.dtype)
        intra_ref[0, rs, :] = jnp.where(tril, kq[c][C:] * decays[c], 0.0).astype(intra_ref.dtype)
        qd_ref[rs, :] = (q_ref[rs, :] * egcs[c]).astype(qd_ref.dtype)
        kt_ref[rs, :] = (k_ref[rs, :] * jnp.exp(gl - gc)).astype(kt_ref.dtype)
        egl_ref[0, c:c + 1, :] = jnp.exp(gl)


def _dn_prep(q, k, v, bg):
    t = q.shape[0]
    rows = PREP_ROWS
    hb = lambda i, h: (i, h)
    blk = pl.BlockSpec((rows, DN_DK), hb)
    wide_bf = jax.ShapeDtypeStruct((t, DN_QK_W), BF16)
    return pl.pallas_call(
        _dn_prep_kernel,
        grid=(t // rows, DN_HEADS),
        in_specs=[blk, blk, blk, pl.BlockSpec((rows, LANES), lambda i, h: (i, 0))],
        out_specs=[blk, blk, blk, blk, pl.BlockSpec((1, rows, C), lambda i, h: (h, i, 0)),
                   pl.BlockSpec((1, rows // C, LANES), lambda i, h: (h, i, 0))],
        out_shape=[wide_bf, wide_bf, wide_bf, jax.ShapeDtypeStruct((t, DN_V_W), F32),
                   jax.ShapeDtypeStruct((DN_HEADS, t, C), BF16),
                   jax.ShapeDtypeStruct((DN_HEADS, t // C, LANES), F32)],
        compiler_params=_params(2),
        name="dn_prep",
    )(q, k, v, bg)


SCAN_ROWS = 512


def _dn_scan_kernel(w_ref, qd_ref, kt_ref, u_ref, intra_ref, egl_ref, o_ref, state_ref):
    @pl.when(pl.program_id(1) == 0)
    def _():
        state_ref[...] = jnp.zeros_like(state_ref)

    rows = w_ref.shape[0]
    for c in range(rows // C):
        rs = slice(c * C, (c + 1) * C)
        for h in range(DN_HEADS):
            hs = slice(h * DN_DK, (h + 1) * DN_DK)
            state = state_ref[h]
            wq = jnp.concatenate([w_ref[rs, hs], qd_ref[rs, hs]], axis=0)
            r1 = _dot(wq, state.astype(BF16))
            v_new = u_ref[rs, hs] - r1[:C]
            vb = v_new.astype(BF16)
            o_ref[rs, hs] = r1[C:] + _dot(intra_ref[h, rs, :], vb)
            state_ref[h] = state * egl_ref[h, c:c + 1, :] + _dot_tn(kt_ref[rs, hs], vb)


def _dn_scan(w, qd, kt, u, intra, egl, batch, seq):
    t = w.shape[0]
    rows = SCAN_ROWS
    steps = seq // rows
    row = lambda b, i: (b * steps + i, 0)
    wide = pl.BlockSpec((rows, DN_QK_W), row)
    return pl.pallas_call(
        _dn_scan_kernel,
        grid=(batch, steps),
        in_specs=[wide, wide, wide, wide,
                  pl.BlockSpec((DN_HEADS, rows, C), lambda b, i: (0, b * steps + i, 0)),
                  pl.BlockSpec((DN_HEADS, rows // C, LANES), lambda b, i: (0, b * steps + i, 0))],
        out_specs=wide,
        out_shape=jax.ShapeDtypeStruct((t, DN_V_W), F32),
        scratch_shapes=[pltpu.VMEM((DN_HEADS, DN_DK, DN_DV), F32)],
        compiler_params=_params(2),
        name="dn_scan",
    )(w, qd, kt, u, intra, egl)


ATT_TQ = 512
ATT_TK = 512
NEG_INF = -1e30


def _mla_attn_kernel(qn_ref, qr_ref, kn_ref, kr_ref, v_ref, o_ref, qcat_ref, kcat_ref):
    seq = qn_ref.shape[0]
    tq, tk = ATT_TQ, ATT_TK
    qcat_ref[:, :LANES] = qn_ref[...]
    qcat_ref[:, LANES:] = qr_ref[...]
    kcat_ref[:, :LANES] = kn_ref[...]
    kcat_ref[:, LANES:] = kr_ref[...]
    causal = (lax.broadcasted_iota(jnp.int32, (tq, tk), 1) <= lax.broadcasted_iota(jnp.int32, (tq, tk), 0))

    for i in range(seq // tq):
        q = qcat_ref[i * tq:(i + 1) * tq, :]
        m = l = acc = None
        for j in range(i + 1):
            ks = slice(j * tk, (j + 1) * tk)
            s = _dot_nt(q, kcat_ref[ks, :])
            if j == i:
                s = jnp.where(causal, s, NEG_INF)
            m_cur = jnp.max(s, axis=1, keepdims=True)
            if j == 0:
                m_new = jnp.broadcast_to(m_cur, (tq, LANES))
            else:
                m_new = jnp.maximum(m, m_cur)
            p = jnp.exp2(s - pltpu.repeat(m_new, tk // LANES, 1))
            pv = _dot(p.astype(BF16), v_ref[ks, :])
            p_sum = jnp.sum(p, axis=1, keepdims=True)
            if j == 0:
                l, acc = jnp.broadcast_to(p_sum, (tq, LANES)), pv
            else:
                alpha = jnp.exp2(m - m_new)
                l, acc = alpha * l + p_sum, alpha * acc + pv
            m = m_new
        o_ref[i * tq:(i + 1) * tq, :] = (acc / l).astype(o_ref.dtype)


def _mla_attn(qn, qr, kn, kr, v, batch, seq):
    t = qn.shape[0]
    blk = pl.BlockSpec((seq, LANES), lambda b, h: (b, h))
    return pl.pallas_call(
        _mla_attn_kernel,
        grid=(batch, MLA_HEADS),
        in_specs=[blk, blk, blk, pl.BlockSpec((seq, LANES), lambda b, h: (b, 0)), blk],
        out_specs=blk,
        out_shape=jax.ShapeDtypeStruct((t, MLA_HEADS * MLA_V), BF16),
        scratch_shapes=[pltpu.VMEM((seq, 2 * LANES), BF16), pltpu.VMEM((seq, 2 * LANES), BF16)],
        compiler_params=_params(2),
        name="mla_attn",
    )(qn, qr, kn, kr, v)


MERGE_TM = 256


def _merge_kernel(x_ref, odn_ref, omla_ref, wzg_ref, wdn_ref, wmla_ref, wo_ref, nw_ref, g_ref, b_ref, h_ref):
    x = x_ref[...]
    zg = _dot(x.astype(BF16), wzg_ref[...])
    z = zg[:, :DN_V_W]
    parts = []
    for h in range(DN_HEADS):
        hs = slice(h * DN_DV, (h + 1) * DN_DV)
        parts.append(_rms(odn_ref[:, hs], nw_ref[...]))
    o_dn = jnp.concatenate(parts, axis=1) * (z * _sigmoid(z))
    y_dn = _dot(o_dn.astype(BF16), wdn_ref[...])
    y_mla = _dot(omla_ref[...], wmla_ref[...])
    mixed = _sigmoid(zg[:, DN_V_W:DN_V_W + D_MODEL]) * y_dn + _sigmoid(zg[:, DN_V_W + D_MODEL:]) * y_mla
    r = DEEPNORM_ALPHA * x + _dot(mixed.astype(BF16), wo_ref[...])
    h_ref[...] = _layer_norm(r, g_ref[...], b_ref[...])


def _merge(x2, o_dn, o_mla, wzg, wdn, wmla, wo, nw, g, b):
    t = x2.shape[0]
    tm = MERGE_TM
    row = lambda i: (i, 0)
    act = pl.BlockSpec((tm, D_MODEL), row)
    return pl.pallas_call(
        _merge_kernel,
        grid=(t // tm,),
        in_specs=[act, act, act, _const_spec(wzg.shape), _const_spec(wdn.shape), _const_spec(wmla.shape),
                  _const_spec(wo.shape), _const_spec(nw.shape), _const_spec(g.shape), _const_spec(b.shape)],
        out_specs=act,
        out_shape=jax.ShapeDtypeStruct((t, D_MODEL), F32),
        compiler_params=_params(1),
        name="merge",
    )(x2, o_dn, o_mla, wzg, wdn, wmla, wo, nw, g, b)


FFN_TM = 512
FFN_SLAB = 256


def _ffn_kernel(h_ref, p_ref, wg_ref, wu_ref, wout_ref, wpg_ref, wple_ref, g_ref, b_ref, o_ref):
    h = h_ref[...]
    hb = h.astype(BF16)
    acc = jnp.zeros(h.shape, F32)
    for j in range(FFN_HIDDEN // FFN_SLAB):
        cs = slice(j * FFN_SLAB, (j + 1) * FFN_SLAB)
        gt = _dot(hb, wg_ref[:, cs])
        up = _dot(hb, wu_ref[:, cs])
        a = (gt * _sigmoid(gt) * up).astype(BF16)
        acc = acc + _dot(a, wout_ref[cs, :])
    ple = _sigmoid(_dot(hb, wpg_ref[...])) * _dot(p_ref[...].astype(BF16), wple_ref[...])
    r = DEEPNORM_ALPHA * h + acc + ple
    o_ref[...] = _layer_norm(r, g_ref[...], b_ref[...])


def _ffn(h1, p2, wg, wu, wout, wpg, wple, g, b):
    t = h1.shape[0]
    tm = FFN_TM
    row = lambda i: (i, 0)
    act = pl.BlockSpec((tm, D_MODEL), row)
    return pl.pallas_call(
        _ffn_kernel,
        grid=(t // tm,),
        in_specs=[act, pl.BlockSpec((tm, PLE_DIM), row), _const_spec(wg.shape), _const_spec(wu.shape),
                  _const_spec(wout.shape), _const_spec(wpg.shape), _const_spec(wple.shape),
                  _const_spec(g.shape), _const_spec(b.shape)],
        out_specs=act,
        out_shape=jax.ShapeDtypeStruct((t, D_MODEL), F32),
        compiler_params=_params(1),
        name="ffn",
    )(h1, p2, wg, wu, wout, wpg, wple, g, b)


def _split_w_in(w_in):
    offs = [0]
    for n in IN_SIZES:
        offs.append(offs[-1] + n)
    return [w_in[:, offs[i]:offs[i + 1]] for i in range(len(IN_SIZES))]


def kernel(x, p, positions, w_in, conv_w, dn_a_log, dn_dt_bias, dn_norm_w, q_norm_w, w_uq, kv_norm_w, w_uk, w_uv,
           w_br_dn, w_br_mla, w_o, ln1_g, ln1_b, w_ffn_in, w_ffn_out, w_ple, w_ple_gate, ln2_g, ln2_b):
    assert w_in.shape[0] == DEPTH
    batch, seq, _ = x.shape
    t = batch * seq
    h = x.reshape(t, D_MODEL)
    pos2 = positions.reshape(t, 1)
    half = MLA_ROPE // 2
    invf = ROPE_BASE ** (-jnp.arange(0, MLA_ROPE, 2, dtype=F32) / MLA_ROPE)
    invf = jnp.tile(invf, LANES // half).reshape(1, LANES)

    for i in range(DEPTH):
        w_qkv, w_z, w_b, w_a, w_cq, w_ckv, w_kr, w_gdn, w_gmla = _split_w_in(w_in[i])
        w_kr_sw = jnp.concatenate([w_kr[:, half:], w_kr[:, :half]], axis=1)
        pad = jnp.zeros((D_MODEL, LAT_W - LAT_BA - 2 * DN_HEADS), F32)
        w_lat = jnp.concatenate([w_cq, w_ckv, w_kr, w_kr_sw, w_b, w_a, pad], axis=1).astype(BF16)
        w_zg = jnp.concatenate([w_z, w_gdn, w_gmla], axis=1).astype(BF16)
        uq = w_uq[i]
        t1 = uq[:, :, MLA_NOPE:MLA_NOPE + half]
        t2 = uq[:, :, MLA_NOPE + half:]
        wqn = uq[:, :, :MLA_NOPE].reshape(MLA_Q_LORA, MLA_HEADS * MLA_NOPE).astype(BF16)
        wqr = jnp.concatenate([t1, t2, t2, t1], axis=2).reshape(MLA_Q_LORA, MLA_HEADS * LANES).astype(BF16)
        wuk = w_uk[i].reshape(MLA_KV_LORA, MLA_HEADS * MLA_NOPE).astype(BF16)
        wuv = w_uv[i].reshape(MLA_KV_LORA, MLA_HEADS * MLA_V).astype(BF16)
        lane_pad = jnp.zeros((LANES - 2 * DN_HEADS,), F32)
        head_pad = jnp.zeros((DN_HEADS,), F32)
        dnp = jnp.stack([jnp.concatenate([head_pad, dn_a_log[i], lane_pad]),
                         jnp.concatenate([head_pad, dn_dt_bias[i], lane_pad])])
        nw = jnp.tile(dn_norm_w[i], 1).reshape(1, DN_DV)
        w_ffn = w_ffn_in[i].astype(BF16)

        q, k, v = _dn_qkv(h, w_qkv.astype(BF16), conv_w[i], seq)
        qn, qr, kn, kr, vm, bg = _latent(h, w_lat, pos2, invf, q_norm_w[i].reshape(1, -1),
                                         kv_norm_w[i].reshape(1, -1), wqn, wqr, wuk, wuv, dnp)
        w, qd, kt, u, intra, egl = _dn_prep(q, k, v, bg)
        o_dn = _dn_scan(w, qd, kt, u, intra, egl, batch, seq)
        o_mla = _mla_attn(qn, qr, kn, kr, vm, batch, seq)
        h1 = _merge(h, o_dn, o_mla, w_zg, w_br_dn[i].astype(BF16), w_br_mla[i].astype(BF16),
                    w_o[i].astype(BF16), nw, ln1_g[i].reshape(1, -1), ln1_b[i].reshape(1, -1))
        h = _ffn(h1, p[i].reshape(t, PLE_DIM), w_ffn[:, :FFN_HIDDEN], w_ffn[:, FFN_HIDDEN:],
                 w_ffn_out[i].astype(BF16), w_ple_gate[i].astype(BF16), w_ple[i].astype(BF16),
                 ln2_g[i].reshape(1, -1), ln2_b[i].reshape(1, -1))
    return h.reshape(batch, seq, D_MODEL)
```

```python
import functools
import math

import jax
import jax.numpy as jnp
from jax import lax
from jax.experimental import pallas as pl
from jax.experimental.pallas import tpu as pltpu

D_MODEL = 1024
PLE_DIM = 256
DN_HEADS = 8
DN_DK = 128
DN_DV = 128
DN_CONV = 4
DN_CHUNK = 64
DN_QK_W = DN_HEADS * DN_DK
DN_V_W = DN_HEADS * DN_DV
DN_CONV_CH = 2 * DN_QK_W + DN_V_W
MLA_HEADS = 8
MLA_Q_LORA = 384
MLA_KV_LORA = 256
MLA_NOPE = 128
MLA_ROPE = 64
MLA_V = 128
ROPE_BASE = 10000.0
FFN_HIDDEN = -(-8 * D_MODEL // (3 * 256)) * 256
DEPTH = 1
DEEPNORM_ALPHA = (2.0 * DEPTH) ** 0.25
IN_SIZES = (DN_CONV_CH, DN_V_W, DN_HEADS, DN_HEADS, MLA_Q_LORA, MLA_KV_LORA, MLA_ROPE, D_MODEL, D_MODEL)

LANES = 128
SUBLANES = 8
VMEM_LIMIT_BYTES = 56 * 1024 * 1024

LAT_W = 896
LAT_CKV = MLA_Q_LORA
LAT_KR = LAT_CKV + MLA_KV_LORA
LAT_BA = LAT_KR + 2 * MLA_ROPE

BF16 = jnp.bfloat16
F32 = jnp.float32


def _dot(a, b):
    return jnp.dot(a, b, preferred_element_type=F32)


def _dot_nt(a, b):
    return lax.dot_general(a, b, (((1,), (1,)), ((), ())), preferred_element_type=F32)


def _dot_tn(a, b):
    return lax.dot_general(a, b, (((0,), (0,)), ((), ())), preferred_element_type=F32)


def _sigmoid(x):
    return 0.5 * jnp.tanh(0.5 * x) + 0.5


def _silu(x):
    h = 0.5 * x
    return h * jnp.tanh(h) + h


def _layer_norm(r, g, b):
    mu = jnp.mean(r, axis=-1, keepdims=True)
    c = r - mu
    var = jnp.mean(c * c, axis=-1, keepdims=True)
    return c * lax.rsqrt(var + 1e-5) * g + b


def _const_spec(shape):
    return pl.BlockSpec(shape, lambda *_: (0,) * len(shape))


def _params(n_axes, sem=None):
    return pltpu.CompilerParams(
        dimension_semantics=sem or ("arbitrary",) * n_axes, vmem_limit_bytes=VMEM_LIMIT_BYTES)


QKV_TM = 512
QKV_SLAB = 512


def _dn_qkv_kernel(tiles_per_seq, x_ref, w_ref, cw_ref, q_ref, k_ref, v_ref, carry_ref):
    i = pl.program_id(0)
    tm = x_ref.shape[0]

    @pl.when(i % tiles_per_seq == 0)
    def _():
        carry_ref[:SUBLANES, :] = jnp.zeros((SUBLANES, DN_CONV_CH), F32)

    xb = x_ref[...].astype(BF16)
    outs = (q_ref, k_ref, v_ref)
    slabs_per_out = DN_QK_W // QKV_SLAB
    for j in range(DN_CONV_CH // QKV_SLAB):
        cs = slice(j * QKV_SLAB, (j + 1) * QKV_SLAB)
        pre = _dot(xb, w_ref[:, cs])
        carry_ref[SUBLANES:, cs] = pre
        cw = cw_ref[:, cs]
        y = pre * cw[3:4]
        for tap in range(DN_CONV - 1):
            shift = DN_CONV - 1 - tap
            y = y + carry_ref[SUBLANES - shift:SUBLANES - shift + tm, cs] * cw[tap:tap + 1]
        carry_ref[:SUBLANES, cs] = pre[tm - SUBLANES:]
        y = _silu(y)
        out_ref = outs[j // slabs_per_out]
        col0 = (j % slabs_per_out) * QKV_SLAB
        which = j // slabs_per_out
        for hh in range(QKV_SLAB // DN_DK):
            yh = y[:, hh * DN_DK:(hh + 1) * DN_DK]
            if which < 2:
                ss = jnp.sum(yh * yh, axis=-1, keepdims=True)
                yh = yh * lax.rsqrt(ss + 1e-6)
                if which == 0:
                    yh = yh * (DN_DK ** -0.5)
            out_ref[:, col0 + hh * DN_DK:col0 + (hh + 1) * DN_DK] = yh.astype(out_ref.dtype)


def _dn_qkv(x2, w_qkv, conv_w, seq):
    t = x2.shape[0]
    tm = QKV_TM
    out = jax.ShapeDtypeStruct((t, DN_QK_W), BF16)
    row = lambda i: (i, 0)
    return pl.pallas_call(
        functools.partial(_dn_qkv_kernel, seq // tm),
        grid=(t // tm,),
        in_specs=[pl.BlockSpec((tm, D_MODEL), row), _const_spec((D_MODEL, DN_CONV_CH)),
                  _const_spec((DN_CONV, DN_CONV_CH))],
        out_specs=[pl.BlockSpec((tm, DN_QK_W), row)] * 3,
        out_shape=[out] * 3,
        scratch_shapes=[pltpu.VMEM((SUBLANES + tm, DN_CONV_CH), F32)],
        compiler_params=_params(1),
        name="dn_qkv",
    )(x2, w_qkv, conv_w)


LAT_TM = 512
MLA_SCALE = (MLA_NOPE + MLA_ROPE) ** -0.5 * math.log2(math.e)


def _rms(t, w):
    return t * lax.rsqrt(jnp.mean(t * t, axis=-1, keepdims=True) + 1e-6) * w


def _latent_kernel(x_ref, wl_ref, pos_ref, invf_ref, qnw_ref, kvw_ref, wqn_ref, wqr_ref, wuk_ref, wuv_ref,
                   dnp_ref, qn_ref, qr_ref, kn_ref, kr_ref, v_ref, bg_ref):
    xb = x_ref[...].astype(BF16)
    lat = _dot(xb, wl_ref[...])

    lane = lax.broadcasted_iota(jnp.int32, (1, LANES), 1)
    ang = pos_ref[...].astype(F32) * invf_ref[...]
    cos_t = jnp.where(lane < MLA_ROPE, jnp.cos(ang), 0.0)
    sin_a = jnp.sin(ang)
    half = MLA_ROPE // 2
    sin_t = jnp.where(lane < half, -sin_a, jnp.where(lane < MLA_ROPE, sin_a, 0.0))

    def rope(a):
        return a * cos_t + pltpu.roll(a, MLA_ROPE, axis=1) * sin_t

    c_q = _rms(lat[:, :MLA_Q_LORA], qnw_ref[...]).astype(BF16)
    qn_ref[...] = (_dot(c_q, wqn_ref[...]) * MLA_SCALE).astype(qn_ref.dtype)
    qr_raw = _dot(c_q, wqr_ref[...])
    for h in range(MLA_HEADS):
        hs = slice(h * LANES, (h + 1) * LANES)
        qr_ref[:, hs] = (rope(qr_raw[:, hs]) * MLA_SCALE).astype(qr_ref.dtype)

    c_kv = _rms(lat[:, LAT_CKV:LAT_KR], kvw_ref[...]).astype(BF16)
    kn_ref[...] = _dot(c_kv, wuk_ref[...]).astype(kn_ref.dtype)
    v_ref[...] = _dot(c_kv, wuv_ref[...]).astype(v_ref.dtype)
    kr_ref[...] = rope(lat[:, LAT_KR:LAT_BA]).astype(kr_ref.dtype)

    ba = lat[:, LAT_BA:LAT_BA + LANES]
    beta = _sigmoid(ba)
    sp_in = ba + dnp_ref[1:2, :]
    softplus = jnp.maximum(sp_in, 0.0) + jnp.log(1.0 + jnp.exp(-jnp.abs(sp_in)))
    g = -jnp.exp(dnp_ref[0:1, :]) * softplus
    bg_ref[...] = jnp.where(lane < DN_HEADS, beta, jnp.where(lane < 2 * DN_HEADS, g, 0.0))


def _latent(x2, w_lat, pos2, invf, qnw, kvw, wqn, wqr, wuk, wuv, dnp):
    t = x2.shape[0]
    tm = LAT_TM
    row = lambda i: (i, 0)
    wide = jax.ShapeDtypeStruct((t, MLA_HEADS * LANES), BF16)
    return pl.pallas_call(
        _latent_kernel,
        grid=(t // tm,),
        in_specs=[pl.BlockSpec((tm, D_MODEL), row), _const_spec(w_lat.shape), pl.BlockSpec((tm, 1), row),
                  _const_spec(invf.shape), _const_spec(qnw.shape), _const_spec(kvw.shape),
                  _const_spec(wqn.shape), _const_spec(wqr.shape), _const_spec(wuk.shape), _const_spec(wuv.shape),
                  _const_spec(dnp.shape)],
        out_specs=[pl.BlockSpec((tm, MLA_HEADS * LANES), row)] * 3 + [pl.BlockSpec((tm, LANES), row)]
        + [pl.BlockSpec((tm, MLA_HEADS * LANES), row), pl.BlockSpec((tm, LANES), row)],
        out_shape=[wide, wide, wide, jax.ShapeDtypeStruct((t, LANES), BF16), wide,
                   jax.ShapeDtypeStruct((t, LANES), F32)],
        compiler_params=_params(1),
        name="latent",
    )(x2, w_lat, pos2, invf, qnw, kvw, wqn, wqr, wuk, wuv, dnp)


PREP_ROWS = 512
PREP_HEADS = 4
C = DN_CHUNK


def _split3(a):
    a1 = a.astype(BF16)
    r1 = a - a1.astype(F32)
    a2 = r1.astype(BF16)
    a3 = (r1 - a2.astype(F32)).astype(BF16)
    return a1, a2, a3


def _split2(a):
    hi = a.astype(BF16)
    return hi, (a - hi.astype(F32)).astype(BF16)


def _pair_blockdiag(m):
    lane = lax.broadcasted_iota(jnp.int32, m.shape, 1)
    zero = jnp.zeros_like(m)
    return jnp.concatenate([jnp.where(lane < C, m, zero), jnp.where(lane >= C, m, zero)], axis=0)


def _pair_mm_split(a, b):
    ah, al = _split2(a)
    bh, bl = _split2(b)
    bdh = _pair_blockdiag(bh)
    lhs = jnp.concatenate([ah, ah, al], axis=1)
    rhs = jnp.concatenate([bdh, _pair_blockdiag(bl), bdh], axis=0)
    return _dot(lhs, rhs)


def _dn_prep_kernel(q_ref, k_ref, v_ref, bg_ref, w_ref, qd_ref, kt_ref, u_ref, intra_ref, egl_ref):
    head0 = pl.program_id(1) * PREP_HEADS
    rows = q_ref.shape[0]
    nc = rows // C
    pc = 2 * C

    row = lax.broadcasted_iota(jnp.int32, (C, pc), 0)
    lane = lax.broadcasted_iota(jnp.int32, (C, pc), 1)
    col = lane % C
    is_a = lane < C
    tril = row >= col
    strict = row > col
    eye = (row == col).astype(F32)
    tril3 = (lax.broadcasted_iota(jnp.int32, (C, 3 * C), 1) % C
             <= lax.broadcasted_iota(jnp.int32, (C, 3 * C), 0)).astype(BF16)
    pick3 = (lane < 3).astype(BF16)

    lane_r = lax.broadcasted_iota(jnp.int32, (3 * LANES, LANES), 0) % LANES
    bg3 = jnp.concatenate(_split3(bg_ref[...]), axis=1)
    betas = [_dot(bg3, (lane_r == head0 + hh).astype(BF16)) for hh in range(PREP_HEADS)]
    gs = [_dot(bg3, (lane_r == head0 + hh + DN_HEADS).astype(BF16)) for hh in range(PREP_HEADS)]

    chunks = [slice(c * C, (c + 1) * C) for c in range(nc)]
    units = [(hh, p) for hh in range(PREP_HEADS) for p in range(nc // 2)]
    gcs = [[_dot(tril3, jnp.concatenate(_split3(g[rs, :]), axis=0)) for rs in chunks] for g in gs]
    mixes = []
    for per_head in gcs:
        mixes.append([])
        for gc in per_head:
            g1, g2, g3 = _split3(gc)
            mixes[-1].append(jnp.where(lane == 0, g1, jnp.where(lane == 1, g2, g3)))
    rowforms = [_dot_nt(pick3, jnp.concatenate([mixes[hh][2 * p], mixes[hh][2 * p + 1]], axis=0))
                for hh, p in units]
    decays, kbs, l_mats, qks = [], [], [], []
    for (hh, p), rowform in zip(units, rowforms):
        rs, hs = slice(p * pc, (p + 1) * pc), slice(hh * DN_DK, (hh + 1) * DN_DK)
        diff = jnp.where(is_a, gcs[hh][2 * p], gcs[hh][2 * p + 1]) - rowform
        decay = jnp.where(tril, jnp.exp(jnp.where(tril, diff, 0.0)), 0.0)
        k = k_ref[rs, hs]
        kb = k * betas[hh][rs, :]
        kbb, qb = kb.astype(BF16), q_ref[rs, hs].astype(BF16)
        res = _dot_nt(jnp.concatenate([kbb[:C], qb[:C], kbb[C:], qb[C:]], axis=0), k.astype(BF16))
        decays.append(decay)
        kbs.append(kb)
        l_mats.append(jnp.where(strict, jnp.where(is_a, res[:C], res[2 * C:3 * C]) * decay, 0.0))
        qks.append(jnp.where(is_a, res[C:2 * C], res[3 * C:]))

    invs = [eye] * len(units)
    s = 1
    while s < C:
        blk = ((row // (2 * s)) == (col // (2 * s))) & ((row % (2 * s)) >= s) & ((col % (2 * s)) < s)
        offs = [jnp.where(blk, l, 0.0) for l in l_mats]
        if s == 1:
            invs = [inv - off for inv, off in zip(invs, offs)]
        else:
            mids = [_pair_mm_split(off, inv) for off, inv in zip(offs, invs)]
            invs = [inv - _pair_mm_split(inv, mid) for inv, mid in zip(invs, mids)]
        s *= 2

    for n, (hh, p) in enumerate(units):
        rs, hs = slice(p * pc, (p + 1) * pc), slice(hh * DN_DK, (hh + 1) * DN_DK)
        ga, gb = gcs[hh][2 * p], gcs[hh][2 * p + 1]
        egc = jnp.exp(jnp.concatenate([ga, gb], axis=0))
        rhs = jnp.concatenate([(v_ref[rs, hs] * betas[hh][rs, :]).astype(BF16), (kbs[n] * egc).astype(BF16)], axis=1)
        uw = _dot(_pair_blockdiag(invs[n].astype(BF16)), rhs)
        u_ref[rs, hs] = uw[:, :DN_DV]
        w_ref[rs, hs] = uw[:, DN_DV:].astype(w_ref.dtype)
        intra_ref[hh, p * C:(p + 1) * C, :] = jnp.where(tril, qks[n] * decays[n], 0.0).astype(intra_ref.dtype)
        qd_ref[rs, hs] = (q_ref[rs, hs] * egc).astype(qd_ref.dtype)
        gla, glb = ga[C - 1:C, :], gb[C - 1:C, :]
        tail = jnp.exp(jnp.concatenate([gla - ga, glb - gb], axis=0))
        kt_ref[rs, hs] = (k_ref[rs, hs] * tail).astype(kt_ref.dtype)
        egl_ref[hh, 2 * p:2 * p + 1, :] = jnp.exp(gla)
        egl_ref[hh, 2 * p + 1:2 * p + 2, :] = jnp.exp(glb)


def _dn_prep(q, k, v, bg):
    t = q.shape[0]
    rows = PREP_ROWS
    blk = pl.BlockSpec((rows, PREP_HEADS * DN_DK), lambda i, h: (i, h))
    wide_bf = jax.ShapeDtypeStruct((t, DN_QK_W), BF16)
    return pl.pallas_call(
        _dn_prep_kernel,
        grid=(t // rows, DN_HEADS // PREP_HEADS),
        in_specs=[blk, blk, blk, pl.BlockSpec((rows, LANES), lambda i, h: (i, 0))],
        out_specs=[blk, blk, blk, blk, pl.BlockSpec((PREP_HEADS, rows // 2, 2 * C), lambda i, h: (h, i, 0)),
                   pl.BlockSpec((PREP_HEADS, rows // C, LANES), lambda i, h: (h, i, 0))],
        out_shape=[wide_bf, wide_bf, wide_bf, jax.ShapeDtypeStruct((t, DN_V_W), F32),
                   jax.ShapeDtypeStruct((DN_HEADS, t // 2, 2 * C), BF16),
                   jax.ShapeDtypeStruct((DN_HEADS, t // C, LANES), F32)],
        compiler_params=_params(2),
        name="dn_prep",
    )(q, k, v, bg)


SCAN_ROWS = 512


def _dn_scan_kernel(w_ref, qd_ref, kt_ref, u_ref, intra_ref, egl_ref, o_ref, state_ref):
    @pl.when(pl.program_id(1) == 0)
    def _():
        state_ref[...] = jnp.zeros_like(state_ref)

    rows = w_ref.shape[0]
    heads = [slice(h * DN_DK, (h + 1) * DN_DK) for h in range(DN_HEADS)]
    states = [state_ref[h] for h in range(DN_HEADS)]
    for c in range(rows // C):
        rs = slice(c * C, (c + 1) * C)
        r1s = [_dot(jnp.concatenate([w_ref[rs, hs], qd_ref[rs, hs]], axis=0), st.astype(BF16))
               for hs, st in zip(heads, states)]
        vbs = [(u_ref[rs, hs] - r1[:C]).astype(BF16) for hs, r1 in zip(heads, r1s)]
        ps = slice((c // 2) * C, (c // 2 + 1) * C)
        zeros = jnp.zeros((C, DN_DV), BF16)
        vps = [jnp.concatenate([vb, zeros] if c % 2 == 0 else [zeros, vb], axis=0) for vb in vbs]
        outs = [r1[C:] + _dot(intra_ref[h, ps, :], vp) for h, (r1, vp) in enumerate(zip(r1s, vps))]
        upds = [_dot_tn(kt_ref[rs, hs], vb) for hs, vb in zip(heads, vbs)]
        for h, hs in enumerate(heads):
            o_ref[rs, hs] = outs[h]
        states = [st * egl_ref[h, c:c + 1, :] + upd for h, (st, upd) in enumerate(zip(states, upds))]
    for h in range(DN_HEADS):
        state_ref[h] = states[h]


def _dn_scan(w, qd, kt, u, intra, egl, batch, seq):
    t = w.shape[0]
    rows = SCAN_ROWS
    steps = seq // rows
    row = lambda b, i: (b * steps + i, 0)
    wide = pl.BlockSpec((rows, DN_QK_W), row)
    return pl.pallas_call(
        _dn_scan_kernel,
        grid=(batch, steps),
        in_specs=[wide, wide, wide, wide,
                  pl.BlockSpec((DN_HEADS, rows // 2, 2 * C), lambda b, i: (0, b * steps + i, 0)),
                  pl.BlockSpec((DN_HEADS, rows // C, LANES), lambda b, i: (0, b * steps + i, 0))],
        out_specs=wide,
        out_shape=jax.ShapeDtypeStruct((t, DN_V_W), F32),
        scratch_shapes=[pltpu.VMEM((DN_HEADS, DN_DK, DN_DV), F32)],
        compiler_params=_params(2),
        name="dn_scan",
    )(w, qd, kt, u, intra, egl)


ATT_TQ = 512
ATT_TK = 512
NEG_INF = -1e30


def _mla_attn_kernel(qn_ref, qr_ref, kn_ref, kr_ref, v_ref, o_ref, qcat_ref, kcat_ref):
    seq = qn_ref.shape[0]
    tq, tk = ATT_TQ, ATT_TK
    qcat_ref[:, :LANES] = qn_ref[...]
    qcat_ref[:, LANES:] = qr_ref[...]
    kcat_ref[:, :LANES] = kn_ref[...]
    kcat_ref[:, LANES:] = kr_ref[...]
    causal = (lax.broadcasted_iota(jnp.int32, (tq, tk), 1) <= lax.broadcasted_iota(jnp.int32, (tq, tk), 0))

    for i in range(seq // tq):
        q = qcat_ref[i * tq:(i + 1) * tq, :]
        m = l = acc = None
        for j in range(i + 1):
            ks = slice(j * tk, (j + 1) * tk)
            s = _dot_nt(q, kcat_ref[ks, :])
            if j == i:
                s = jnp.where(causal, s, NEG_INF)
            m_cur = jnp.max(s, axis=1, keepdims=True)
            if j == 0:
                m_new = jnp.broadcast_to(m_cur, (tq, LANES))
            else:
                m_new = jnp.maximum(m, m_cur)
            p = jnp.exp2(s - jnp.concatenate([m_new] * (tk // LANES), axis=1))
            pv = _dot(p.astype(BF16), v_ref[ks, :])
            p_sum = jnp.sum(p, axis=1, keepdims=True)
            if j == 0:
                l, acc = jnp.broadcast_to(p_sum, (tq, LANES)), pv
            else:
                alpha = jnp.exp2(m - m_new)
                l, acc = alpha * l + p_sum, alpha * acc + pv
            m = m_new
        o_ref[i * tq:(i + 1) * tq, :] = (acc / l).astype(o_ref.dtype)


def _mla_attn(qn, qr, kn, kr, v, batch, seq):
    t = qn.shape[0]
    blk = pl.BlockSpec((seq, LANES), lambda b, h: (b, h))
    return pl.pallas_call(
        _mla_attn_kernel,
        grid=(batch, MLA_HEADS),
        in_specs=[blk, blk, blk, pl.BlockSpec((seq, LANES), lambda b, h: (b, 0)), blk],
        out_specs=blk,
        out_shape=jax.ShapeDtypeStruct((t, MLA_HEADS * MLA_V), BF16),
        scratch_shapes=[pltpu.VMEM((seq, 2 * LANES), BF16), pltpu.VMEM((seq, 2 * LANES), BF16)],
        compiler_params=_params(2),
        name="mla_attn",
    )(qn, qr, kn, kr, v)


MERGE_TM = 256


def _merge_kernel(x_ref, odn_ref, omla_ref, wzg_ref, wdn_ref, wmla_ref, wo_ref, nw_ref, g_ref, b_ref, h_ref):
    x = x_ref[...]
    zg = _dot(x.astype(BF16), wzg_ref[...])
    z = zg[:, :DN_V_W]
    parts = []
    for h in range(DN_HEADS):
        hs = slice(h * DN_DV, (h + 1) * DN_DV)
        parts.append(_rms(odn_ref[:, hs], nw_ref[...]))
    o_dn = jnp.concatenate(parts, axis=1) * _silu(z)
    y_dn = _dot(o_dn.astype(BF16), wdn_ref[...])
    y_mla = _dot(omla_ref[...], wmla_ref[...])
    mixed = _sigmoid(zg[:, DN_V_W:DN_V_W + D_MODEL]) * y_dn + _sigmoid(zg[:, DN_V_W + D_MODEL:]) * y_mla
    r = DEEPNORM_ALPHA * x + _dot(mixed.astype(BF16), wo_ref[...])
    h_ref[...] = _layer_norm(r, g_ref[...], b_ref[...])


def _merge(x2, o_dn, o_mla, wzg, wdn, wmla, wo, nw, g, b):
    t = x2.shape[0]
    tm = MERGE_TM
    row = lambda i: (i, 0)
    act = pl.BlockSpec((tm, D_MODEL), row)
    return pl.pallas_call(
        _merge_kernel,
        grid=(t // tm,),
        in_specs=[act, act, act, _const_spec(wzg.shape), _const_spec(wdn.shape), _const_spec(wmla.shape),
                  _const_spec(wo.shape), _const_spec(nw.shape), _const_spec(g.shape), _const_spec(b.shape)],
        out_specs=act,
        out_shape=jax.ShapeDtypeStruct((t, D_MODEL), F32),
        compiler_params=_params(1),
        name="merge",
    )(x2, o_dn, o_mla, wzg, wdn, wmla, wo, nw, g, b)


FFN_TM = 512
FFN_SLAB = 256


def _ffn_kernel(h_ref, p_ref, wg_ref, wu_ref, wout_ref, wpg_ref, wple_ref, g_ref, b_ref, o_ref):
    h = h_ref[...]
    hb = h.astype(BF16)
    acc = jnp.zeros(h.shape, F32)
    for j in range(FFN_HIDDEN // FFN_SLAB):
        cs = slice(j * FFN_SLAB, (j + 1) * FFN_SLAB)
        gt = _dot(hb, wg_ref[:, cs])
        up = _dot(hb, wu_ref[:, cs])
        a = (_silu(gt) * up).astype(BF16)
        acc = acc + _dot(a, wout_ref[cs, :])
    ple = _sigmoid(_dot(hb, wpg_ref[...])) * _dot(p_ref[...].astype(BF16), wple_ref[...])
    r = DEEPNORM_ALPHA * h + acc + ple
    o_ref[...] = _layer_norm(r, g_ref[...], b_ref[...])


def _ffn(h1, p2, wg, wu, wout, wpg, wple, g, b):
    t = h1.shape[0]
    tm = FFN_TM
    row = lambda i: (i, 0)
    act = pl.BlockSpec((tm, D_MODEL), row)
    return pl.pallas_call(
        _ffn_kernel,
        grid=(t // tm,),
        in_specs=[act, pl.BlockSpec((tm, PLE_DIM), row), _const_spec(wg.shape), _const_spec(wu.shape),
                  _const_spec(wout.shape), _const_spec(wpg.shape), _const_spec(wple.shape),
                  _const_spec(g.shape), _const_spec(b.shape)],
        out_specs=act,
        out_shape=jax.ShapeDtypeStruct((t, D_MODEL), F32),
        compiler_params=_params(1),
        name="ffn",
    )(h1, p2, wg, wu, wout, wpg, wple, g, b)


def _split_w_in(w_in):
    offs = [0]
    for n in IN_SIZES:
        offs.append(offs[-1] + n)
    return [w_in[:, offs[i]:offs[i + 1]] for i in range(len(IN_SIZES))]


def kernel(x, p, positions, w_in, conv_w, dn_a_log, dn_dt_bias, dn_norm_w, q_norm_w, w_uq, kv_norm_w, w_uk, w_uv,
           w_br_dn, w_br_mla, w_o, ln1_g, ln1_b, w_ffn_in, w_ffn_out, w_ple, w_ple_gate, ln2_g, ln2_b):
    assert w_in.shape[0] == DEPTH
    batch, seq, _ = x.shape
    t = batch * seq
    h = x.reshape(t, D_MODEL)
    pos2 = positions.reshape(t, 1)
    half = MLA_ROPE // 2
    invf = ROPE_BASE ** (-jnp.arange(0, MLA_ROPE, 2, dtype=F32) / MLA_ROPE)
    invf = jnp.tile(invf, LANES // half).reshape(1, LANES)

    for i in range(DEPTH):
        w_qkv, w_z, w_b, w_a, w_cq, w_ckv, w_kr, w_gdn, w_gmla = _split_w_in(w_in[i])
        w_kr_sw = jnp.concatenate([w_kr[:, half:], w_kr[:, :half]], axis=1)
        pad = jnp.zeros((D_MODEL, LAT_W - LAT_BA - 2 * DN_HEADS), F32)
        w_lat = jnp.concatenate([w_cq, w_ckv, w_kr, w_kr_sw, w_b, w_a, pad], axis=1).astype(BF16)
        w_zg = jnp.concatenate([w_z, w_gdn, w_gmla], axis=1).astype(BF16)
        uq = w_uq[i]
        t1 = uq[:, :, MLA_NOPE:MLA_NOPE + half]
        t2 = uq[:, :, MLA_NOPE + half:]
        wqn = uq[:, :, :MLA_NOPE].reshape(MLA_Q_LORA, MLA_HEADS * MLA_NOPE).astype(BF16)
        wqr = jnp.concatenate([t1, t2, t2, t1], axis=2).reshape(MLA_Q_LORA, MLA_HEADS * LANES).astype(BF16)
        wuk = w_uk[i].reshape(MLA_KV_LORA, MLA_HEADS * MLA_NOPE).astype(BF16)
        wuv = w_uv[i].reshape(MLA_KV_LORA, MLA_HEADS * MLA_V).astype(BF16)
        lane_pad = jnp.zeros((LANES - 2 * DN_HEADS,), F32)
        head_pad = jnp.zeros((DN_HEADS,), F32)
        dnp = jnp.stack([jnp.concatenate([head_pad, dn_a_log[i], lane_pad]),
                         jnp.concatenate([head_pad, dn_dt_bias[i], lane_pad])])
        nw = jnp.tile(dn_norm_w[i], 1).reshape(1, DN_DV)
        w_ffn = w_ffn_in[i].astype(BF16)

        q, k, v = _dn_qkv(h, w_qkv.astype(BF16), conv_w[i], seq)
        qn, qr, kn, kr, vm, bg = _latent(h, w_lat, pos2, invf, q_norm_w[i].reshape(1, -1),
                                         kv_norm_w[i].reshape(1, -1), wqn, wqr, wuk, wuv, dnp)
        w, qd, kt, u, intra, egl = _dn_prep(q, k, v, bg)
        o_dn = _dn_scan(w, qd, kt, u, intra, egl, batch, seq)
        o_mla = _mla_attn(qn, qr, kn, kr, vm, batch, seq)
        h1 = _merge(h, o_dn, o_mla, w_zg, w_br_dn[i].astype(BF16), w_br_mla[i].astype(BF16),
                    w_o[i].astype(BF16), nw, ln1_g[i].reshape(1, -1), ln1_b[i].reshape(1, -1))
        h = _ffn(h1, p[i].reshape(t, PLE_DIM), w_ffn[:, :FFN_HIDDEN], w_ffn[:, FFN_HIDDEN:],
                 w_ffn_out[i].astype(BF16), w_ple_gate[i].astype(BF16), w_ple[i].astype(BF16),
                 ln2_g[i].reshape(1, -1), ln2_b[i].reshape(1, -1))
    return h.reshape(batch, seq, D_MODEL)
```

```python
import functools
import math

import jax
import jax.numpy as jnp
from jax import lax
from jax.experimental import pallas as pl
from jax.experimental.pallas import tpu as pltpu

D_MODEL = 1024
PLE_DIM = 256
DN_HEADS = 8
DN_DK = 128
DN_DV = 128
DN_CONV = 4
DN_CHUNK = 64
DN_QK_W = DN_HEADS * DN_DK
DN_V_W = DN_HEADS * DN_DV
DN_CONV_CH = 2 * DN_QK_W + DN_V_W
MLA_HEADS = 8
MLA_Q_LORA = 384
MLA_KV_LORA = 256
MLA_NOPE = 128
MLA_ROPE = 64
MLA_V = 128
ROPE_BASE = 10000.0
FFN_HIDDEN = -(-8 * D_MODEL // (3 * 256)) * 256
DEPTH = 1
DEEPNORM_ALPHA = (2.0 * DEPTH) ** 0.25
IN_SIZES = (DN_CONV_CH, DN_V_W, DN_HEADS, DN_HEADS, MLA_Q_LORA, MLA_KV_LORA, MLA_ROPE, D_MODEL, D_MODEL)

LANES = 128
SUBLANES = 8
VMEM_LIMIT_BYTES = 56 * 1024 * 1024

LAT_W = 896
LAT_CKV = MLA_Q_LORA
LAT_KR = LAT_CKV + MLA_KV_LORA
LAT_BA = LAT_KR + 2 * MLA_ROPE

BF16 = jnp.bfloat16
F32 = jnp.float32


def _dot(a, b):
    return jnp.dot(a, b, preferred_element_type=F32)


def _dot_nt(a, b):
    return lax.dot_general(a, b, (((1,), (1,)), ((), ())), preferred_element_type=F32)


def _dot_tn(a, b):
    return lax.dot_general(a, b, (((0,), (0,)), ((), ())), preferred_element_type=F32)


def _sigmoid(x):
    return 0.5 * jnp.tanh(0.5 * x) + 0.5


def _silu(x):
    h = 0.5 * x
    return h * jnp.tanh(h) + h


def _layer_norm(r, g, b):
    mu = jnp.mean(r, axis=-1, keepdims=True)
    c = r - mu
    var = jnp.mean(c * c, axis=-1, keepdims=True)
    return c * lax.rsqrt(var + 1e-5) * g + b


def _const_spec(shape):
    return pl.BlockSpec(shape, lambda *_: (0,) * len(shape))


def _params(n_axes, sem=None):
    return pltpu.CompilerParams(
        dimension_semantics=sem or ("arbitrary",) * n_axes, vmem_limit_bytes=VMEM_LIMIT_BYTES)


QKV_TM = 512
QKV_SLAB = 512
QKV_ROWS = 64


def _dn_qkv_kernel(tiles_per_seq, x_ref, w_ref, cw_ref, q_ref, k_ref, v_ref, pre_ref, tail_ref):
    i = pl.program_id(0)
    tm = x_ref.shape[0]
    assert DN_CONV == 4

    @pl.when(i % tiles_per_seq == 0)
    def _():
        pre_ref[:SUBLANES, :] = jnp.zeros((SUBLANES, DN_CONV_CH), F32)
        tail_ref[...] = jnp.zeros_like(tail_ref)

    xb = x_ref[...].astype(BF16)
    outs = (q_ref, k_ref, v_ref)
    slabs_per_out = DN_QK_W // QKV_SLAB
    for j in range(DN_CONV_CH // QKV_SLAB):
        cs = slice(j * QKV_SLAB, (j + 1) * QKV_SLAB)
        pre_ref[SUBLANES:, cs] = _dot(xb, w_ref[:, cs])
        cw = cw_ref[:, cs]
        out_ref = outs[j // slabs_per_out]
        col0 = (j % slabs_per_out) * QKV_SLAB
        which = j // slabs_per_out
        pair_tail = tail_ref[:, cs]
        for r in range(tm // QKV_ROWS):
            ext = pre_ref[r * QKV_ROWS:(r + 1) * QKV_ROWS + SUBLANES, cs]
            pre = ext[SUBLANES:]
            prev = pltpu.roll(ext, 1, axis=0)[SUBLANES:]
            pair = pre * cw[1:2] + prev * cw[0:1]
            pair2 = pltpu.roll(jnp.concatenate([pair_tail, pair], axis=0), 2, axis=0)[SUBLANES:]
            pair_tail = pair[QKV_ROWS - SUBLANES:]
            y = _silu(pre * cw[3:4] + prev * cw[2:3] + pair2)
            rs = slice(r * QKV_ROWS, (r + 1) * QKV_ROWS)
            for hh in range(QKV_SLAB // DN_DK):
                yh = y[:, hh * DN_DK:(hh + 1) * DN_DK]
                if which < 2:
                    ss = jnp.sum(yh * yh, axis=-1, keepdims=True)
                    scale = lax.rsqrt(ss + 1e-6)
                    yh = yh * (scale * (DN_DK ** -0.5) if which == 0 else scale)
                out_ref[rs, col0 + hh * DN_DK:col0 + (hh + 1) * DN_DK] = yh.astype(out_ref.dtype)
        tail_ref[:, cs] = pair_tail
        pre_ref[:SUBLANES, cs] = pre_ref[tm:, cs]


def _dn_qkv(x2, w_qkv, conv_w, seq):
    t = x2.shape[0]
    tm = QKV_TM
    out = jax.ShapeDtypeStruct((t, DN_QK_W), BF16)
    row = lambda i: (i, 0)
    return pl.pallas_call(
        functools.partial(_dn_qkv_kernel, seq // tm),
        grid=(t // tm,),
        in_specs=[pl.BlockSpec((tm, D_MODEL), row), _const_spec((D_MODEL, DN_CONV_CH)),
                  _const_spec((DN_CONV, DN_CONV_CH))],
        out_specs=[pl.BlockSpec((tm, DN_QK_W), row)] * 3,
        out_shape=[out] * 3,
        scratch_shapes=[pltpu.VMEM((SUBLANES + tm, DN_CONV_CH), F32), pltpu.VMEM((SUBLANES, DN_CONV_CH), F32)],
        compiler_params=_params(1),
        name="dn_qkv",
    )(x2, w_qkv, conv_w)


LAT_TM = 512
MLA_SCALE = (MLA_NOPE + MLA_ROPE) ** -0.5 * math.log2(math.e)


def _rms(t, w):
    return t * lax.rsqrt(jnp.mean(t * t, axis=-1, keepdims=True) + 1e-6) * w


def _latent_kernel(x_ref, wl_ref, pos_ref, invf_ref, qnw_ref, kvw_ref, wqn_ref, wqr_ref, wuk_ref, wuv_ref,
                   dnp_ref, qn_ref, qr_ref, kn_ref, kr_ref, v_ref, bg_ref):
    xb = x_ref[...].astype(BF16)
    lat = _dot(xb, wl_ref[...])

    tm = x_ref.shape[0]
    qt = tm // 4
    half = MLA_ROPE // 2
    lane = lax.broadcasted_iota(jnp.int32, (1, LANES), 1)
    pos = pos_ref[...].astype(F32)
    grp = lane // half
    pos4 = jnp.where(grp == 0, pos[:qt], jnp.where(grp == 1, pos[qt:2 * qt],
                                                   jnp.where(grp == 2, pos[2 * qt:3 * qt], pos[3 * qt:])))
    ang = pos4 * invf_ref[...]
    cos3 = jnp.concatenate(_split3(jnp.cos(ang)), axis=1)
    sin3 = jnp.concatenate(_split3(jnp.sin(ang)), axis=1)
    src = lax.broadcasted_iota(jnp.int32, (3 * LANES, LANES), 0) % LANES
    dst = lax.broadcasted_iota(jnp.int32, (3 * LANES, LANES), 1)
    cos_parts, sin_parts = [], []
    for g in range(4):
        hit = (src // half == g) & (src % half == dst % half) & (dst < MLA_ROPE)
        cos_parts.append(_dot(cos3, jnp.where(hit, 1.0, 0.0).astype(BF16)))
        sin_parts.append(_dot(sin3, jnp.where(hit, jnp.where(dst < half, -1.0, 1.0), 0.0).astype(BF16)))
    cos_t = jnp.concatenate(cos_parts, axis=0)
    sin_t = jnp.concatenate(sin_parts, axis=0)

    def rope(a):
        return a * cos_t + pltpu.roll(a, MLA_ROPE, axis=1) * sin_t

    c_q = _rms(lat[:, :MLA_Q_LORA], qnw_ref[...]).astype(BF16)
    qn_ref[...] = (_dot(c_q, wqn_ref[...]) * MLA_SCALE).astype(qn_ref.dtype)
    qr_raw = _dot(c_q, wqr_ref[...])
    for h in range(MLA_HEADS):
        hs = slice(h * LANES, (h + 1) * LANES)
        qr_ref[:, hs] = (rope(qr_raw[:, hs]) * MLA_SCALE).astype(qr_ref.dtype)

    c_kv = _rms(lat[:, LAT_CKV:LAT_KR], kvw_ref[...]).astype(BF16)
    kn_ref[...] = _dot(c_kv, wuk_ref[...]).astype(kn_ref.dtype)
    v_ref[...] = _dot(c_kv, wuv_ref[...]).astype(v_ref.dtype)
    kr_ref[...] = rope(lat[:, LAT_KR:LAT_BA]).astype(kr_ref.dtype)

    ba = lat[:, LAT_BA:LAT_BA + LANES]
    beta = _sigmoid(ba)
    sp_in = ba + dnp_ref[1:2, :]
    softplus = jnp.maximum(sp_in, 0.0) + jnp.log(1.0 + jnp.exp(-jnp.abs(sp_in)))
    g = -jnp.exp(dnp_ref[0:1, :]) * softplus
    bg_ref[...] = jnp.where(lane < DN_HEADS, beta, jnp.where(lane < 2 * DN_HEADS, g, 0.0))


def _latent(x2, w_lat, pos2, invf, qnw, kvw, wqn, wqr, wuk, wuv, dnp):
    t = x2.shape[0]
    tm = LAT_TM
    row = lambda i: (i, 0)
    wide = jax.ShapeDtypeStruct((t, MLA_HEADS * LANES), BF16)
    return pl.pallas_call(
        _latent_kernel,
        grid=(t // tm,),
        in_specs=[pl.BlockSpec((tm, D_MODEL), row), _const_spec(w_lat.shape), pl.BlockSpec((tm, 1), row),
                  _const_spec(invf.shape), _const_spec(qnw.shape), _const_spec(kvw.shape),
                  _const_spec(wqn.shape), _const_spec(wqr.shape), _const_spec(wuk.shape), _const_spec(wuv.shape),
                  _const_spec(dnp.shape)],
        out_specs=[pl.BlockSpec((tm, MLA_HEADS * LANES), row)] * 3 + [pl.BlockSpec((tm, LANES), row)]
        + [pl.BlockSpec((tm, MLA_HEADS * LANES), row), pl.BlockSpec((tm, LANES), row)],
        out_shape=[wide, wide, wide, jax.ShapeDtypeStruct((t, LANES), BF16), wide,
                   jax.ShapeDtypeStruct((t, LANES), F32)],
        compiler_params=_params(1),
        name="latent",
    )(x2, w_lat, pos2, invf, qnw, kvw, wqn, wqr, wuk, wuv, dnp)


PREP_ROWS = 512
PREP_HEADS = 4
C = DN_CHUNK


def _split3(a):
    a1 = a.astype(BF16)
    r1 = a - a1.astype(F32)
    a2 = r1.astype(BF16)
    a3 = (r1 - a2.astype(F32)).astype(BF16)
    return a1, a2, a3


def _split2(a):
    hi = a.astype(BF16)
    return hi, (a - hi.astype(F32)).astype(BF16)


def _pair_blockdiag(m):
    lane = lax.broadcasted_iota(jnp.int32, m.shape, 1)
    zero = jnp.zeros_like(m)
    return jnp.concatenate([jnp.where(lane < C, m, zero), jnp.where(lane >= C, m, zero)], axis=0)


def _pair_mm_split(a, b, split_a=True):
    ah, al = _split2(a)
    bh, bl = _split2(b)
    bdh = _pair_blockdiag(bh)
    lhs = [ah, ah, al] if split_a else [ah, ah]
    rhs = [bdh, _pair_blockdiag(bl), bdh] if split_a else [bdh, _pair_blockdiag(bl)]
    return _dot(jnp.concatenate(lhs, axis=1), jnp.concatenate(rhs, axis=0))


def _dn_prep_kernel(q_ref, k_ref, v_ref, bg_ref, w_ref, qd_ref, kt_ref, u_ref, intra_ref, egl_ref):
    head0 = pl.program_id(1) * PREP_HEADS
    rows = q_ref.shape[0]
    nc = rows // C
    pc = 2 * C

    row = lax.broadcasted_iota(jnp.int32, (C, pc), 0)
    lane = lax.broadcasted_iota(jnp.int32, (C, pc), 1)
    col = lane % C
    is_a = lane < C
    tril = row >= col
    strict = row > col
    eye = (row == col).astype(F32)
    tril3 = (lax.broadcasted_iota(jnp.int32, (C, 3 * C), 1) % C
             <= lax.broadcasted_iota(jnp.int32, (C, 3 * C), 0)).astype(BF16)
    pick3 = (lane < 3).astype(BF16)

    lane_r = lax.broadcasted_iota(jnp.int32, (3 * LANES, LANES), 0) % LANES
    bg3 = jnp.concatenate(_split3(bg_ref[...]), axis=1)
    betas = [_dot(bg3, (lane_r == head0 + hh).astype(BF16)) for hh in range(PREP_HEADS)]
    gs = [_dot(bg3, (lane_r == head0 + hh + DN_HEADS).astype(BF16)) for hh in range(PREP_HEADS)]

    chunks = [slice(c * C, (c + 1) * C) for c in range(nc)]
    units = [(hh, p) for hh in range(PREP_HEADS) for p in range(nc // 2)]
    gcs = [[_dot(tril3, jnp.concatenate(_split3(g[rs, :]), axis=0)) for rs in chunks] for g in gs]
    mixes = []
    for per_head in gcs:
        mixes.append([])
        for gc in per_head:
            g1, g2, g3 = _split3(gc)
            mixes[-1].append(jnp.where(lane == 0, g1, jnp.where(lane == 1, g2, g3)))
    rowforms = [_dot_nt(pick3, jnp.concatenate([mixes[hh][2 * p], mixes[hh][2 * p + 1]], axis=0))
                for hh, p in units]
    decays, kbs, l_mats, qks = [], [], [], []
    for (hh, p), rowform in zip(units, rowforms):
        rs, hs = slice(p * pc, (p + 1) * pc), slice(hh * DN_DK, (hh + 1) * DN_DK)
        diff = jnp.where(is_a, gcs[hh][2 * p], gcs[hh][2 * p + 1]) - rowform
        decay = jnp.where(tril, jnp.exp(jnp.where(tril, diff, 0.0)), 0.0)
        k = k_ref[rs, hs]
        kb = k * betas[hh][rs, :]
        kbb, qb = kb.astype(BF16), q_ref[rs, hs].astype(BF16)
        res = _dot_nt(jnp.concatenate([kbb[:C], qb[:C], kbb[C:], qb[C:]], axis=0), k.astype(BF16))
        decays.append(decay)
        kbs.append(kb)
        l_mats.append(jnp.where(strict, jnp.where(is_a, res[:C], res[2 * C:3 * C]) * decay, 0.0))
        qks.append(jnp.where(is_a, res[C:2 * C], res[3 * C:]))

    invs = [eye] * len(units)
    s = 1
    while s < C:
        blk = ((row // (2 * s)) == (col // (2 * s))) & ((row % (2 * s)) >= s) & ((col % (2 * s)) < s)
        offs = [jnp.where(blk, l, 0.0) for l in l_mats]
        if s == 1:
            invs = [inv - off for inv, off in zip(invs, offs)]
        else:
            mids = [_pair_mm_split(off, inv, split_a=False) for off, inv in zip(offs, invs)]
            invs = [inv - _pair_mm_split(inv, mid) for inv, mid in zip(invs, mids)]
        s *= 2

    for n, (hh, p) in enumerate(units):
        rs, hs = slice(p * pc, (p + 1) * pc), slice(hh * DN_DK, (hh + 1) * DN_DK)
        ga, gb = gcs[hh][2 * p], gcs[hh][2 * p + 1]
        egc = jnp.exp(jnp.concatenate([ga, gb], axis=0))
        rhs = jnp.concatenate([(v_ref[rs, hs] * betas[hh][rs, :]).astype(BF16), (kbs[n] * egc).astype(BF16)], axis=1)
        uw = _dot(_pair_blockdiag(invs[n].astype(BF16)), rhs)
        u_ref[rs, hs] = uw[:, :DN_DV]
        w_ref[rs, hs] = uw[:, DN_DV:].astype(w_ref.dtype)
        intra_ref[hh, p * C:(p + 1) * C, :] = jnp.where(tril, qks[n] * decays[n], 0.0).astype(intra_ref.dtype)
        qd_ref[rs, hs] = (q_ref[rs, hs] * egc).astype(qd_ref.dtype)
        gla, glb = ga[C - 1:C, :], gb[C - 1:C, :]
        tail = jnp.exp(jnp.concatenate([gla - ga, glb - gb], axis=0))
        kt_ref[rs, hs] = (k_ref[rs, hs] * tail).astype(kt_ref.dtype)
        egl_ref[hh, 2 * p:2 * p + 1, :] = jnp.exp(gla)
        egl_ref[hh, 2 * p + 1:2 * p + 2, :] = jnp.exp(glb)


def _dn_prep(q, k, v, bg):
    t = q.shape[0]
    rows = PREP_ROWS
    blk = pl.BlockSpec((rows, PREP_HEADS * DN_DK), lambda i, h: (i, h))
    wide_bf = jax.ShapeDtypeStruct((t, DN_QK_W), BF16)
    return pl.pallas_call(
        _dn_prep_kernel,
        grid=(t // rows, DN_HEADS // PREP_HEADS),
        in_specs=[blk, blk, blk, pl.BlockSpec((rows, LANES), lambda i, h: (i, 0))],
        out_specs=[blk, blk, blk, blk, pl.BlockSpec((PREP_HEADS, rows // 2, 2 * C), lambda i, h: (h, i, 0)),
                   pl.BlockSpec((PREP_HEADS, rows // C, LANES), lambda i, h: (h, i, 0))],
        out_shape=[wide_bf, wide_bf, wide_bf, jax.ShapeDtypeStruct((t, DN_V_W), F32),
                   jax.ShapeDtypeStruct((DN_HEADS, t // 2, 2 * C), BF16),
                   jax.ShapeDtypeStruct((DN_HEADS, t // C, LANES), F32)],
        compiler_params=_params(2),
        name="dn_prep",
    )(q, k, v, bg)


SCAN_ROWS = 512


def _dn_scan_kernel(w_ref, qd_ref, kt_ref, u_ref, intra_ref, egl_ref, o_ref, state_ref):
    @pl.when(pl.program_id(1) == 0)
    def _():
        state_ref[...] = jnp.zeros_like(state_ref)

    rows = w_ref.shape[0]
    heads = [slice(h * DN_DK, (h + 1) * DN_DK) for h in range(DN_HEADS)]
    states = [state_ref[h] for h in range(DN_HEADS)]
    for c in range(rows // C):
        rs = slice(c * C, (c + 1) * C)
        r1s = [_dot(jnp.concatenate([w_ref[rs, hs], qd_ref[rs, hs]], axis=0), st.astype(BF16))
               for hs, st in zip(heads, states)]
        vbs = [(u_ref[rs, hs] - r1[:C]).astype(BF16) for hs, r1 in zip(heads, r1s)]
        ps = slice((c // 2) * C, (c // 2 + 1) * C)
        zeros = jnp.zeros((C, DN_DV), BF16)
        vps = [jnp.concatenate([vb, zeros] if c % 2 == 0 else [zeros, vb], axis=0) for vb in vbs]
        outs = [r1[C:] + _dot(intra_ref[h, ps, :], vp) for h, (r1, vp) in enumerate(zip(r1s, vps))]
        upds = [_dot_tn(kt_ref[rs, hs], vb) for hs, vb in zip(heads, vbs)]
        for h, hs in enumerate(heads):
            o_ref[rs, hs] = outs[h]
        states = [st * egl_ref[h, c:c + 1, :] + upd for h, (st, upd) in enumerate(zip(states, upds))]
    for h in range(DN_HEADS):
        state_ref[h] = states[h]


def _dn_scan(w, qd, kt, u, intra, egl, batch, seq):
    t = w.shape[0]
    rows = SCAN_ROWS
    steps = seq // rows
    row = lambda b, i: (b * steps + i, 0)
    wide = pl.BlockSpec((rows, DN_QK_W), row)
    return pl.pallas_call(
        _dn_scan_kernel,
        grid=(batch, steps),
        in_specs=[wide, wide, wide, wide,
                  pl.BlockSpec((DN_HEADS, rows // 2, 2 * C), lambda b, i: (0, b * steps + i, 0)),
                  pl.BlockSpec((DN_HEADS, rows // C, LANES), lambda b, i: (0, b * steps + i, 0))],
        out_specs=wide,
        out_shape=jax.ShapeDtypeStruct((t, DN_V_W), F32),
        scratch_shapes=[pltpu.VMEM((DN_HEADS, DN_DK, DN_DV), F32)],
        compiler_params=_params(2),
        name="dn_scan",
    )(w, qd, kt, u, intra, egl)


ATT_TQ = 512
ATT_TK = 512
NEG_INF = -1e30


def _mla_attn_kernel(qn_ref, qr_ref, kn_ref, kr_ref, v_ref, o_ref, qcat_ref, kcat_ref, vcat_ref):
    seq = qn_ref.shape[0]
    tq, tk = ATT_TQ, ATT_TK
    qcat_ref[:, :LANES] = qn_ref[...]
    qcat_ref[:, LANES:] = qr_ref[...]
    kcat_ref[:, :LANES] = kn_ref[...]
    kcat_ref[:, LANES:] = kr_ref[...]
    vcat_ref[:, :LANES] = v_ref[...]
    vcat_ref[:, LANES:] = jnp.ones((seq, LANES), BF16)
    causal = (lax.broadcasted_iota(jnp.int32, (tq, tk), 1) <= lax.broadcasted_iota(jnp.int32, (tq, tk), 0))

    for i in range(seq // tq):
        q = qcat_ref[i * tq:(i + 1) * tq, :]
        m = acc = None
        for j in range(i + 1):
            ks = slice(j * tk, (j + 1) * tk)
            s = _dot_nt(q, kcat_ref[ks, :])
            if j == i:
                s = jnp.where(causal, s, NEG_INF)
            m_cur = jnp.max(s, axis=1, keepdims=True)
            if j == 0:
                m_new = jnp.broadcast_to(m_cur, (tq, LANES))
            else:
                m_new = jnp.maximum(m, m_cur)
            p = jnp.exp2(s - jnp.concatenate([m_new] * (tk // LANES), axis=1))
            pv = _dot(p.astype(BF16), vcat_ref[ks, :])
            if j == 0:
                acc = pv
            else:
                alpha = jnp.exp2(m - m_new)
                acc = jnp.concatenate([alpha, alpha], axis=1) * acc + pv
            m = m_new
        o_ref[i * tq:(i + 1) * tq, :] = (acc[:, :LANES] / acc[:, LANES:]).astype(o_ref.dtype)


def _mla_attn(qn, qr, kn, kr, v, batch, seq):
    t = qn.shape[0]
    blk = pl.BlockSpec((seq, LANES), lambda b, h: (b, h))
    return pl.pallas_call(
        _mla_attn_kernel,
        grid=(batch, MLA_HEADS),
        in_specs=[blk, blk, blk, pl.BlockSpec((seq, LANES), lambda b, h: (b, 0)), blk],
        out_specs=blk,
        out_shape=jax.ShapeDtypeStruct((t, MLA_HEADS * MLA_V), BF16),
        scratch_shapes=[pltpu.VMEM((seq, 2 * LANES), BF16)] * 3,
        compiler_params=_params(2),
        name="mla_attn",
    )(qn, qr, kn, kr, v)


MERGE_TM = 512


def _merge_kernel(x_ref, odn_ref, omla_ref, wzg_ref, wdn_ref, wmla_ref, wo_ref, nw_ref, g_ref, b_ref, h_ref):
    x = x_ref[...]
    zg = _dot(x.astype(BF16), wzg_ref[...])
    z = zg[:, :DN_V_W]
    parts = []
    for h in range(DN_HEADS):
        hs = slice(h * DN_DV, (h + 1) * DN_DV)
        parts.append(_rms(odn_ref[:, hs], nw_ref[...]))
    o_dn = jnp.concatenate(parts, axis=1) * _silu(z)
    y_dn = _dot(o_dn.astype(BF16), wdn_ref[...])
    y_mla = _dot(omla_ref[...], wmla_ref[...])
    mixed = _sigmoid(zg[:, DN_V_W:DN_V_W + D_MODEL]) * y_dn + _sigmoid(zg[:, DN_V_W + D_MODEL:]) * y_mla
    r = DEEPNORM_ALPHA * x + _dot(mixed.astype(BF16), wo_ref[...])
    h_ref[...] = _layer_norm(r, g_ref[...], b_ref[...])


def _merge(x2, o_dn, o_mla, wzg, wdn, wmla, wo, nw, g, b):
    t = x2.shape[0]
    tm = MERGE_TM
    row = lambda i: (i, 0)
    act = pl.BlockSpec((tm, D_MODEL), row)
    return pl.pallas_call(
        _merge_kernel,
        grid=(t // tm,),
        in_specs=[act, act, act, _const_spec(wzg.shape), _const_spec(wdn.shape), _const_spec(wmla.shape),
                  _const_spec(wo.shape), _const_spec(nw.shape), _const_spec(g.shape), _const_spec(b.shape)],
        out_specs=act,
        out_shape=jax.ShapeDtypeStruct((t, D_MODEL), F32),
        compiler_params=_params(1),
        name="merge",
    )(x2, o_dn, o_mla, wzg, wdn, wmla, wo, nw, g, b)


FFN_TM = 512
FFN_SLAB = 256


def _ffn_kernel(h_ref, p_ref, win_ref, wout_ref, wpg_ref, wple_ref, g_ref, b_ref, o_ref):
    h = h_ref[...]
    hb = h.astype(BF16)
    acc = jnp.zeros(h.shape, F32)
    for j in range(FFN_HIDDEN // FFN_SLAB):
        cs = slice(j * FFN_SLAB, (j + 1) * FFN_SLAB)
        gt = _dot(hb, win_ref[:, cs])
        up = _dot(hb, win_ref[:, FFN_HIDDEN + j * FFN_SLAB:FFN_HIDDEN + (j + 1) * FFN_SLAB])
        a = (_silu(gt) * up).astype(BF16)
        acc = acc + _dot(a, wout_ref[cs, :])
    ple = _sigmoid(_dot(hb, wpg_ref[...])) * _dot(p_ref[...].astype(BF16), wple_ref[...])
    r = DEEPNORM_ALPHA * h + acc + ple
    o_ref[...] = _layer_norm(r, g_ref[...], b_ref[...])


def _ffn(h1, p2, win, wout, wpg, wple, g, b):
    assert FFN_HIDDEN % FFN_SLAB == 0
    t = h1.shape[0]
    tm = FFN_TM
    row = lambda i: (i, 0)
    act = pl.BlockSpec((tm, D_MODEL), row)
    return pl.pallas_call(
        _ffn_kernel,
        grid=(t // tm,),
        in_specs=[act, pl.BlockSpec((tm, PLE_DIM), row), _const_spec(win.shape), _const_spec(wout.shape),
                  _const_spec(wpg.shape), _const_spec(wple.shape), _const_spec(g.shape), _const_spec(b.shape)],
        out_specs=act,
        out_shape=jax.ShapeDtypeStruct((t, D_MODEL), F32),
        compiler_params=_params(1),
        name="ffn",
    )(h1, p2, win, wout, wpg, wple, g, b)


def _split_w_in(w_in):
    offs = [0]
    for n in IN_SIZES:
        offs.append(offs[-1] + n)
    return [w_in[:, offs[i]:offs[i + 1]] for i in range(len(IN_SIZES))]


def kernel(x, p, positions, w_in, conv_w, dn_a_log, dn_dt_bias, dn_norm_w, q_norm_w, w_uq, kv_norm_w, w_uk, w_uv,
           w_br_dn, w_br_mla, w_o, ln1_g, ln1_b, w_ffn_in, w_ffn_out, w_ple, w_ple_gate, ln2_g, ln2_b):
    assert w_in.shape[0] == DEPTH
    batch, seq, _ = x.shape
    t = batch * seq
    h = x.reshape(t, D_MODEL)
    pos2 = positions.reshape(t, 1)
    half = MLA_ROPE // 2
    invf = ROPE_BASE ** (-jnp.arange(0, MLA_ROPE, 2, dtype=F32) / MLA_ROPE)
    invf = jnp.tile(invf, LANES // half).reshape(1, LANES)

    for i in range(DEPTH):
        w_qkv, w_z, w_b, w_a, w_cq, w_ckv, w_kr, w_gdn, w_gmla = _split_w_in(w_in[i])
        w_kr_sw = jnp.concatenate([w_kr[:, half:], w_kr[:, :half]], axis=1)
        pad = jnp.zeros((D_MODEL, LAT_W - LAT_BA - 2 * DN_HEADS), F32)
        w_lat = jnp.concatenate([w_cq, w_ckv, w_kr, w_kr_sw, w_b, w_a, pad], axis=1).astype(BF16)
        w_zg = jnp.concatenate([w_z, w_gdn, w_gmla], axis=1).astype(BF16)
        uq = w_uq[i]
        t1 = uq[:, :, MLA_NOPE:MLA_NOPE + half]
        t2 = uq[:, :, MLA_NOPE + half:]
        wqn = uq[:, :, :MLA_NOPE].reshape(MLA_Q_LORA, MLA_HEADS * MLA_NOPE).astype(BF16)
        wqr = jnp.concatenate([t1, t2, t2, t1], axis=2).reshape(MLA_Q_LORA, MLA_HEADS * LANES).astype(BF16)
        wuk = w_uk[i].reshape(MLA_KV_LORA, MLA_HEADS * MLA_NOPE).astype(BF16)
        wuv = w_uv[i].reshape(MLA_KV_LORA, MLA_HEADS * MLA_V).astype(BF16)
        lane_pad = jnp.zeros((LANES - 2 * DN_HEADS,), F32)
        head_pad = jnp.zeros((DN_HEADS,), F32)
        dnp = jnp.stack([jnp.concatenate([head_pad, dn_a_log[i], lane_pad]),
                         jnp.concatenate([head_pad, dn_dt_bias[i], lane_pad])])
        nw = jnp.tile(dn_norm_w[i], 1).reshape(1, DN_DV)
        w_ffn = w_ffn_in[i].astype(BF16)

        q, k, v = _dn_qkv(h, w_qkv.astype(BF16), conv_w[i], seq)
        qn, qr, kn, kr, vm, bg = _latent(h, w_lat, pos2, invf, q_norm_w[i].reshape(1, -1),
                                         kv_norm_w[i].reshape(1, -1), wqn, wqr, wuk, wuv, dnp)
        w, qd, kt, u, intra, egl = _dn_prep(q, k, v, bg)
        o_dn = _dn_scan(w, qd, kt, u, intra, egl, batch, seq)
        o_mla = _mla_attn(qn, qr, kn, kr, vm, batch, seq)
        h1 = _merge(h, o_dn, o_mla, w_zg, w_br_dn[i].astype(BF16), w_br_mla[i].astype(BF16),
                    w_o[i].astype(BF16), nw, ln1_g[i].reshape(1, -1), ln1_b[i].reshape(1, -1))
        h = _ffn(h1, p[i].reshape(t, PLE_DIM), w_ffn,
                 w_ffn_out[i].astype(BF16), w_ple_gate[i].astype(BF16), w_ple[i].astype(BF16),
                 ln2_g[i].reshape(1, -1), ln2_b[i].reshape(1, -1))
    return h.reshape(batch, seq, D_MODEL)
```

```python
import functools
import math

import jax
import jax.numpy as jnp
from jax import lax
from jax.experimental import pallas as pl
from jax.experimental.pallas import tpu as pltpu

D_MODEL = 1024
PLE_DIM = 256
DN_HEADS = 8
DN_DK = 128
DN_DV = 128
DN_CONV = 4
DN_CHUNK = 64
DN_QK_W = DN_HEADS * DN_DK
DN_V_W = DN_HEADS * DN_DV
DN_CONV_CH = 2 * DN_QK_W + DN_V_W
MLA_HEADS = 8
MLA_Q_LORA = 384
MLA_KV_LORA = 256
MLA_NOPE = 128
MLA_ROPE = 64
MLA_V = 128
ROPE_BASE = 10000.0
FFN_HIDDEN = -(-8 * D_MODEL // (3 * 256)) * 256
DEPTH = 1
DEEPNORM_ALPHA = (2.0 * DEPTH) ** 0.25
IN_SIZES = (DN_CONV_CH, DN_V_W, DN_HEADS, DN_HEADS, MLA_Q_LORA, MLA_KV_LORA, MLA_ROPE, D_MODEL, D_MODEL)

LANES = 128
SUBLANES = 8
VMEM_LIMIT_BYTES = 56 * 1024 * 1024

LAT_W = 896
LAT_CKV = MLA_Q_LORA
LAT_KR = LAT_CKV + MLA_KV_LORA
LAT_BA = LAT_KR + 2 * MLA_ROPE

BF16 = jnp.bfloat16
F32 = jnp.float32


def _dot(a, b):
    return jnp.dot(a, b, preferred_element_type=F32)


def _dot_nt(a, b):
    return lax.dot_general(a, b, (((1,), (1,)), ((), ())), preferred_element_type=F32)


def _dot_tn(a, b):
    return lax.dot_general(a, b, (((0,), (0,)), ((), ())), preferred_element_type=F32)


def _sigmoid(x):
    return 0.5 * jnp.tanh(0.5 * x) + 0.5


def _silu(x):
    h = 0.5 * x
    return h * jnp.tanh(h) + h


def _layer_norm(r, g, b):
    mu = jnp.mean(r, axis=-1, keepdims=True)
    c = r - mu
    var = jnp.mean(c * c, axis=-1, keepdims=True)
    return c * lax.rsqrt(var + 1e-5) * g + b


def _const_spec(shape):
    return pl.BlockSpec(shape, lambda *_: (0,) * len(shape))


def _params(n_axes, sem=None):
    return pltpu.CompilerParams(
        dimension_semantics=sem or ("arbitrary",) * n_axes, vmem_limit_bytes=VMEM_LIMIT_BYTES)


QKV_TM = 512
QKV_SLAB = 512
QKV_ROWS = 64


def _dn_qkv_kernel(tiles_per_seq, x_ref, w_ref, cw_ref, q_ref, k_ref, v_ref, pre_ref, tail_ref):
    i = pl.program_id(0)
    tm = x_ref.shape[0]
    assert DN_CONV == 4

    @pl.when(i % tiles_per_seq == 0)
    def _():
        pre_ref[:SUBLANES, :] = jnp.zeros((SUBLANES, DN_CONV_CH), F32)
        tail_ref[...] = jnp.zeros_like(tail_ref)

    xb = x_ref[...].astype(BF16)
    outs = (q_ref, k_ref, v_ref)
    slabs_per_out = DN_QK_W // QKV_SLAB
    for j in range(DN_CONV_CH // QKV_SLAB):
        cs = slice(j * QKV_SLAB, (j + 1) * QKV_SLAB)
        pre_ref[SUBLANES:, cs] = _dot(xb, w_ref[:, cs])
        cw = cw_ref[:, cs]
        out_ref = outs[j // slabs_per_out]
        col0 = (j % slabs_per_out) * QKV_SLAB
        which = j // slabs_per_out
        pair_tail = tail_ref[:, cs]
        for r in range(tm // QKV_ROWS):
            ext = pre_ref[r * QKV_ROWS:(r + 1) * QKV_ROWS + SUBLANES, cs]
            pre = ext[SUBLANES:]
            prev = pltpu.roll(ext, 1, axis=0)[SUBLANES:]
            pair = pre * cw[1:2] + prev * cw[0:1]
            pair2 = pltpu.roll(jnp.concatenate([pair_tail, pair], axis=0), 2, axis=0)[SUBLANES:]
            pair_tail = pair[QKV_ROWS - SUBLANES:]
            y = _silu(pre * cw[3:4] + prev * cw[2:3] + pair2)
            rs = slice(r * QKV_ROWS, (r + 1) * QKV_ROWS)
            for hh in range(QKV_SLAB // DN_DK):
                yh = y[:, hh * DN_DK:(hh + 1) * DN_DK]
                if which < 2:
                    ss = jnp.sum(yh * yh, axis=-1, keepdims=True)
                    scale = lax.rsqrt(ss + 1e-6)
                    yh = yh * (scale * (DN_DK ** -0.5) if which == 0 else scale)
                out_ref[rs, col0 + hh * DN_DK:col0 + (hh + 1) * DN_DK] = yh.astype(out_ref.dtype)
        tail_ref[:, cs] = pair_tail
        pre_ref[:SUBLANES, cs] = pre_ref[tm:, cs]


def _dn_qkv(x2, w_qkv, conv_w, seq):
    t = x2.shape[0]
    tm = QKV_TM
    out = jax.ShapeDtypeStruct((t, DN_QK_W), BF16)
    row = lambda i: (i, 0)
    return pl.pallas_call(
        functools.partial(_dn_qkv_kernel, seq // tm),
        grid=(t // tm,),
        in_specs=[pl.BlockSpec((tm, D_MODEL), row), _const_spec((D_MODEL, DN_CONV_CH)),
                  _const_spec((DN_CONV, DN_CONV_CH))],
        out_specs=[pl.BlockSpec((tm, DN_QK_W), row)] * 3,
        out_shape=[out] * 3,
        scratch_shapes=[pltpu.VMEM((SUBLANES + tm, DN_CONV_CH), F32), pltpu.VMEM((SUBLANES, DN_CONV_CH), F32)],
        compiler_params=_params(1),
        name="dn_qkv",
    )(x2, w_qkv, conv_w)


LAT_TM = 512
MLA_SCALE = (MLA_NOPE + MLA_ROPE) ** -0.5 * math.log2(math.e)


def _rms(t, w):
    return t * lax.rsqrt(jnp.mean(t * t, axis=-1, keepdims=True) + 1e-6) * w


def _latent_kernel(x_ref, wl_ref, pos_ref, invf_ref, qnw_ref, kvw_ref, wqn_ref, wqr_ref, wuk_ref, wuv_ref,
                   dnp_ref, qn_ref, qr_ref, kn_ref, kr_ref, v_ref, bg_ref):
    xb = x_ref[...].astype(BF16)
    lat = _dot(xb, wl_ref[...])

    tm = x_ref.shape[0]
    qt = tm // 4
    half = MLA_ROPE // 2
    lane = lax.broadcasted_iota(jnp.int32, (1, LANES), 1)
    pos = pos_ref[...].astype(F32)
    grp = lane // half
    pos4 = jnp.where(grp == 0, pos[:qt], jnp.where(grp == 1, pos[qt:2 * qt],
                                                   jnp.where(grp == 2, pos[2 * qt:3 * qt], pos[3 * qt:])))
    ang = pos4 * invf_ref[...]
    cos3 = jnp.concatenate(_split3(jnp.cos(ang)), axis=1)
    sin3 = jnp.concatenate(_split3(jnp.sin(ang)), axis=1)
    src = lax.broadcasted_iota(jnp.int32, (3 * LANES, LANES), 0) % LANES
    dst = lax.broadcasted_iota(jnp.int32, (3 * LANES, LANES), 1)
    cos_parts, sin_parts = [], []
    for g in range(4):
        hit = (src // half == g) & (src % half == dst % half) & (dst < MLA_ROPE)
        cos_parts.append(_dot(cos3, jnp.where(hit, 1.0, 0.0).astype(BF16)))
        sin_parts.append(_dot(sin3, jnp.where(hit, jnp.where(dst < half, -1.0, 1.0), 0.0).astype(BF16)))
    cos_t = jnp.concatenate(cos_parts, axis=0)
    sin_t = jnp.concatenate(sin_parts, axis=0)

    def rope(a):
        return a * cos_t + pltpu.roll(a, MLA_ROPE, axis=1) * sin_t

    c_q = _rms(lat[:, :MLA_Q_LORA], qnw_ref[...]).astype(BF16)
    qn_ref[...] = (_dot(c_q, wqn_ref[...]) * MLA_SCALE).astype(qn_ref.dtype)
    qr_raw = _dot(c_q, wqr_ref[...])
    for h in range(MLA_HEADS):
        hs = slice(h * LANES, (h + 1) * LANES)
        qr_ref[:, hs] = (rope(qr_raw[:, hs]) * MLA_SCALE).astype(qr_ref.dtype)

    c_kv = _rms(lat[:, LAT_CKV:LAT_KR], kvw_ref[...]).astype(BF16)
    kn_ref[...] = _dot(c_kv, wuk_ref[...]).astype(kn_ref.dtype)
    v_ref[...] = _dot(c_kv, wuv_ref[...]).astype(v_ref.dtype)
    kr_ref[...] = rope(lat[:, LAT_KR:LAT_BA]).astype(kr_ref.dtype)

    ba = lat[:, LAT_BA:LAT_BA + LANES]
    beta = _sigmoid(ba)
    sp_in = ba + dnp_ref[1:2, :]
    softplus = jnp.maximum(sp_in, 0.0) + jnp.log(1.0 + jnp.exp(-jnp.abs(sp_in)))
    g = -jnp.exp(dnp_ref[0:1, :]) * softplus
    bg_ref[...] = jnp.where(lane < DN_HEADS, beta, jnp.where(lane < 2 * DN_HEADS, g, 0.0))


def _latent(x2, w_lat, pos2, invf, qnw, kvw, wqn, wqr, wuk, wuv, dnp):
    t = x2.shape[0]
    tm = LAT_TM
    row = lambda i: (i, 0)
    wide = jax.ShapeDtypeStruct((t, MLA_HEADS * LANES), BF16)
    return pl.pallas_call(
        _latent_kernel,
        grid=(t // tm,),
        in_specs=[pl.BlockSpec((tm, D_MODEL), row), _const_spec(w_lat.shape), pl.BlockSpec((tm, 1), row),
                  _const_spec(invf.shape), _const_spec(qnw.shape), _const_spec(kvw.shape),
                  _const_spec(wqn.shape), _const_spec(wqr.shape), _const_spec(wuk.shape), _const_spec(wuv.shape),
                  _const_spec(dnp.shape)],
        out_specs=[pl.BlockSpec((tm, MLA_HEADS * LANES), row)] * 3 + [pl.BlockSpec((tm, LANES), row)]
        + [pl.BlockSpec((tm, MLA_HEADS * LANES), row), pl.BlockSpec((tm, LANES), row)],
        out_shape=[wide, wide, wide, jax.ShapeDtypeStruct((t, LANES), BF16), wide,
                   jax.ShapeDtypeStruct((t, LANES), F32)],
        compiler_params=_params(1),
        name="latent",
    )(x2, w_lat, pos2, invf, qnw, kvw, wqn, wqr, wuk, wuv, dnp)


PREP_ROWS = 512
PREP_HEADS = 8
C = DN_CHUNK


def _split3(a):
    a1 = a.astype(BF16)
    r1 = a - a1.astype(F32)
    a2 = r1.astype(BF16)
    a3 = (r1 - a2.astype(F32)).astype(BF16)
    return a1, a2, a3


def _pair_blockdiag(m):
    lane = lax.broadcasted_iota(jnp.int32, m.shape, 1)
    zero = jnp.zeros_like(m)
    return jnp.concatenate([jnp.where(lane < C, m, zero), jnp.where(lane >= C, m, zero)], axis=0)


def _pair_mm(a, b):
    return _dot(a.astype(BF16), _pair_blockdiag(b.astype(BF16)))


def _dn_prep_kernel(q_ref, k_ref, v_ref, bg_ref, w_ref, qd_ref, kt_ref, u_ref, intra_ref, egl_ref):
    assert PREP_HEADS == DN_HEADS
    rows = q_ref.shape[0]
    nc = rows // C
    pc = 2 * C

    row = lax.broadcasted_iota(jnp.int32, (C, pc), 0)
    lane = lax.broadcasted_iota(jnp.int32, (C, pc), 1)
    col = lane % C
    is_a = lane < C
    tril = row >= col
    strict = row > col
    eye = (row == col).astype(F32)
    tril3 = (lax.broadcasted_iota(jnp.int32, (C, 3 * C), 1) % C
             <= lax.broadcasted_iota(jnp.int32, (C, 3 * C), 0)).astype(BF16)
    pick3 = (lane < 3).astype(BF16)

    bg = bg_ref[...]
    betas = [jnp.broadcast_to(bg[:, hh:hh + 1], (rows, LANES)) for hh in range(DN_HEADS)]
    gs = [jnp.broadcast_to(bg[:, DN_HEADS + hh:DN_HEADS + hh + 1], (rows, LANES)) for hh in range(DN_HEADS)]

    chunks = [slice(c * C, (c + 1) * C) for c in range(nc)]
    units = [(hh, p) for hh in range(PREP_HEADS) for p in range(nc // 2)]
    gcs = [[_dot(tril3, jnp.concatenate(_split3(g[rs, :]), axis=0)) for rs in chunks] for g in gs]
    mixes = []
    for per_head in gcs:
        mixes.append([])
        for gc in per_head:
            g1, g2, g3 = _split3(gc)
            mixes[-1].append(jnp.where(lane == 0, g1, jnp.where(lane == 1, g2, g3)))
    rowforms = [_dot_nt(pick3, jnp.concatenate([mixes[hh][2 * p], mixes[hh][2 * p + 1]], axis=0))
                for hh, p in units]
    decays, kbs, l_mats, qks = [], [], [], []
    for (hh, p), rowform in zip(units, rowforms):
        rs, hs = slice(p * pc, (p + 1) * pc), slice(hh * DN_DK, (hh + 1) * DN_DK)
        diff = jnp.where(is_a, gcs[hh][2 * p], gcs[hh][2 * p + 1]) - rowform
        decay = jnp.where(tril, jnp.exp(jnp.where(tril, diff, 0.0)), 0.0)
        k = k_ref[rs, hs]
        kb = k * betas[hh][rs, :]
        kbb, qb = kb.astype(BF16), q_ref[rs, hs].astype(BF16)
        res = _dot_nt(jnp.concatenate([kbb[:C], qb[:C], kbb[C:], qb[C:]], axis=0), k.astype(BF16))
        decays.append(decay)
        kbs.append(kb)
        l_mats.append(jnp.where(strict, jnp.where(is_a, res[:C], res[2 * C:3 * C]) * decay, 0.0))
        qks.append(jnp.where(is_a, res[C:2 * C], res[3 * C:]))

    invs = [eye] * len(units)
    s = 1
    while s < C:
        blk = ((row // (2 * s)) == (col // (2 * s))) & ((row % (2 * s)) >= s) & ((col % (2 * s)) < s)
        offs = [jnp.where(blk, l, 0.0) for l in l_mats]
        if s == 1:
            invs = [inv - off for inv, off in zip(invs, offs)]
        else:
            mids = [_pair_mm(off, inv) for off, inv in zip(offs, invs)]
            invs = [inv - _pair_mm(inv, mid) for inv, mid in zip(invs, mids)]
        s *= 2

    for n, (hh, p) in enumerate(units):
        rs, hs = slice(p * pc, (p + 1) * pc), slice(hh * DN_DK, (hh + 1) * DN_DK)
        ga, gb = gcs[hh][2 * p], gcs[hh][2 * p + 1]
        egc = jnp.exp(jnp.concatenate([ga, gb], axis=0))
        rhs = jnp.concatenate([(v_ref[rs, hs] * betas[hh][rs, :]).astype(BF16), (kbs[n] * egc).astype(BF16)], axis=1)
        uw = _dot(_pair_blockdiag(invs[n].astype(BF16)), rhs)
        u_ref[rs, hs] = uw[:, :DN_DV]
        w_ref[rs, hs] = uw[:, DN_DV:].astype(w_ref.dtype)
        intra_ref[hh, p * C:(p + 1) * C, :] = jnp.where(tril, qks[n] * decays[n], 0.0).astype(intra_ref.dtype)
        qd_ref[rs, hs] = (q_ref[rs, hs] * egc).astype(qd_ref.dtype)
        gla, glb = ga[C - 1:C, :], gb[C - 1:C, :]
        tail = jnp.exp(jnp.concatenate([gla - ga, glb - gb], axis=0))
        kt_ref[rs, hs] = (k_ref[rs, hs] * tail).astype(kt_ref.dtype)
        egl_ref[hh, 2 * p:2 * p + 1, :] = jnp.exp(gla)
        egl_ref[hh, 2 * p + 1:2 * p + 2, :] = jnp.exp(glb)


def _dn_prep(q, k, v, bg):
    t = q.shape[0]
    rows = PREP_ROWS
    blk = pl.BlockSpec((rows, PREP_HEADS * DN_DK), lambda i, h: (i, h))
    wide_bf = jax.ShapeDtypeStruct((t, DN_QK_W), BF16)
    return pl.pallas_call(
        _dn_prep_kernel,
        grid=(t // rows, DN_HEADS // PREP_HEADS),
        in_specs=[blk, blk, blk, pl.BlockSpec((rows, LANES), lambda i, h: (i, 0))],
        out_specs=[blk, blk, blk, blk, pl.BlockSpec((PREP_HEADS, rows // 2, 2 * C), lambda i, h: (h, i, 0)),
                   pl.BlockSpec((PREP_HEADS, rows // C, LANES), lambda i, h: (h, i, 0))],
        out_shape=[wide_bf, wide_bf, wide_bf, jax.ShapeDtypeStruct((t, DN_V_W), F32),
                   jax.ShapeDtypeStruct((DN_HEADS, t // 2, 2 * C), BF16),
                   jax.ShapeDtypeStruct((DN_HEADS, t // C, LANES), F32)],
        compiler_params=_params(2),
        name="dn_prep",
    )(q, k, v, bg)


SCAN_ROWS = 512
SCAN_BATCH = 2


def _dn_scan_kernel(w_ref, qd_ref, kt_ref, u_ref, intra_ref, egl_ref, o_ref, state_ref):
    @pl.when(pl.program_id(1) == 0)
    def _():
        state_ref[...] = jnp.zeros_like(state_ref)

    nb, rows = w_ref.shape[0], w_ref.shape[1]
    chains = [(bb, h, slice(h * DN_DK, (h + 1) * DN_DK)) for bb in range(nb) for h in range(DN_HEADS)]
    states = [state_ref[bb, h] for bb, h, _ in chains]
    zeros = jnp.zeros((C, DN_DV), BF16)
    for c in range(rows // C):
        rs = slice(c * C, (c + 1) * C)
        r1s = [_dot(jnp.concatenate([w_ref[bb, rs, hs], qd_ref[bb, rs, hs]], axis=0), st.astype(BF16))
               for (bb, _, hs), st in zip(chains, states)]
        vbs = [(u_ref[bb, rs, hs] - r1[:C]).astype(BF16) for (bb, _, hs), r1 in zip(chains, r1s)]
        ps = slice((c // 2) * C, (c // 2 + 1) * C)
        vps = [jnp.concatenate([vb, zeros] if c % 2 == 0 else [zeros, vb], axis=0) for vb in vbs]
        outs = [r1[C:] + _dot(intra_ref[h, bb, ps, :], vp) for (bb, h, _), r1, vp in zip(chains, r1s, vps)]
        upds = [_dot_tn(kt_ref[bb, rs, hs], vb) for (bb, _, hs), vb in zip(chains, vbs)]
        for (bb, _, hs), out in zip(chains, outs):
            o_ref[bb, rs, hs] = out
        states = [st * egl_ref[h, bb, c:c + 1, :] + upd for (bb, h, _), st, upd in zip(chains, states, upds)]
    for (bb, h, _), st in zip(chains, states):
        state_ref[bb, h] = st


def _dn_scan(w, qd, kt, u, intra, egl, batch, seq):
    rows = SCAN_ROWS
    nb = SCAN_BATCH
    as_bs = lambda a: a.reshape(batch, seq, a.shape[-1])
    wide = pl.BlockSpec((nb, rows, DN_QK_W), lambda b, i: (b, i, 0))
    intra4 = intra.reshape(DN_HEADS, batch, seq // 2, 2 * C)
    egl4 = egl.reshape(DN_HEADS, batch, seq // C, LANES)
    out = pl.pallas_call(
        _dn_scan_kernel,
        grid=(batch // nb, seq // rows),
        in_specs=[wide, wide, wide, wide,
                  pl.BlockSpec((DN_HEADS, nb, rows // 2, 2 * C), lambda b, i: (0, b, i, 0)),
                  pl.BlockSpec((DN_HEADS, nb, rows // C, LANES), lambda b, i: (0, b, i, 0))],
        out_specs=wide,
        out_shape=jax.ShapeDtypeStruct((batch, seq, DN_V_W), F32),
        scratch_shapes=[pltpu.VMEM((nb, DN_HEADS, DN_DK, DN_DV), F32)],
        compiler_params=_params(2),
        name="dn_scan",
    )(as_bs(w), as_bs(qd), as_bs(kt), as_bs(u), intra4, egl4)
    return out.reshape(batch * seq, DN_V_W)


ATT_TQ = 512
ATT_TK = 512
NEG_INF = -1e30


def _mla_attn_kernel(qn_ref, qr_ref, kn_ref, kr_ref, v_ref, o_ref, qcat_ref, kcat_ref, vcat_ref):
    seq = qn_ref.shape[0]
    tq, tk = ATT_TQ, ATT_TK
    qcat_ref[:, :LANES] = qn_ref[...]
    qcat_ref[:, LANES:] = qr_ref[...]
    kcat_ref[:, :LANES] = kn_ref[...]
    kcat_ref[:, LANES:] = kr_ref[...]
    vcat_ref[:, :LANES] = v_ref[...]
    vcat_ref[:, LANES:] = jnp.ones((seq, LANES), BF16)
    causal = (lax.broadcasted_iota(jnp.int32, (tq, tk), 1) <= lax.broadcasted_iota(jnp.int32, (tq, tk), 0))

    for i in range(seq // tq):
        q = qcat_ref[i * tq:(i + 1) * tq, :]
        m = acc = None
        for j in range(i + 1):
            ks = slice(j * tk, (j + 1) * tk)
            s = _dot_nt(q, kcat_ref[ks, :])
            if j == i:
                s = jnp.where(causal, s, NEG_INF)
            m_cur = jnp.max(s, axis=1, keepdims=True)
            if j == 0:
                m_new = jnp.broadcast_to(m_cur, (tq, LANES))
            else:
                m_new = jnp.maximum(m, m_cur)
            p = jnp.exp2(s - jnp.concatenate([m_new] * (tk // LANES), axis=1))
            pv = _dot(p.astype(BF16), vcat_ref[ks, :])
            if j == 0:
                acc = pv
            else:
                alpha = jnp.exp2(m - m_new)
                acc = jnp.concatenate([alpha, alpha], axis=1) * acc + pv
            m = m_new
        o_ref[i * tq:(i + 1) * tq, :] = (acc[:, :LANES] / acc[:, LANES:]).astype(o_ref.dtype)


def _mla_attn(qn, qr, kn, kr, v, batch, seq):
    t = qn.shape[0]
    blk = pl.BlockSpec((seq, LANES), lambda b, h: (b, h))
    return pl.pallas_call(
        _mla_attn_kernel,
        grid=(batch, MLA_HEADS),
        in_specs=[blk, blk, blk, pl.BlockSpec((seq, LANES), lambda b, h: (b, 0)), blk],
        out_specs=blk,
        out_shape=jax.ShapeDtypeStruct((t, MLA_HEADS * MLA_V), BF16),
        scratch_shapes=[pltpu.VMEM((seq, 2 * LANES), BF16)] * 3,
        compiler_params=_params(2),
        name="mla_attn",
    )(qn, qr, kn, kr, v)


MERGE_TM = 512


def _merge_kernel(x_ref, odn_ref, omla_ref, wzg_ref, wdn_ref, wmla_ref, wo_ref, nw_ref, g_ref, b_ref, h_ref):
    x = x_ref[...]
    zg = _dot(x.astype(BF16), wzg_ref[...])
    z = zg[:, :DN_V_W]
    parts = []
    for h in range(DN_HEADS):
        hs = slice(h * DN_DV, (h + 1) * DN_DV)
        parts.append(_rms(odn_ref[:, hs], nw_ref[...]))
    o_dn = jnp.concatenate(parts, axis=1) * _silu(z)
    y_dn = _dot(o_dn.astype(BF16), wdn_ref[...])
    y_mla = _dot(omla_ref[...], wmla_ref[...])
    mixed = _sigmoid(zg[:, DN_V_W:DN_V_W + D_MODEL]) * y_dn + _sigmoid(zg[:, DN_V_W + D_MODEL:]) * y_mla
    r = DEEPNORM_ALPHA * x + _dot(mixed.astype(BF16), wo_ref[...])
    h_ref[...] = _layer_norm(r, g_ref[...], b_ref[...])


def _merge(x2, o_dn, o_mla, wzg, wdn, wmla, wo, nw, g, b):
    t = x2.shape[0]
    tm = MERGE_TM
    row = lambda i: (i, 0)
    act = pl.BlockSpec((tm, D_MODEL), row)
    return pl.pallas_call(
        _merge_kernel,
        grid=(t // tm,),
        in_specs=[act, act, act, _const_spec(wzg.shape), _const_spec(wdn.shape), _const_spec(wmla.shape),
                  _const_spec(wo.shape), _const_spec(nw.shape), _const_spec(g.shape), _const_spec(b.shape)],
        out_specs=act,
        out_shape=jax.ShapeDtypeStruct((t, D_MODEL), F32),
        compiler_params=_params(1),
        name="merge",
    )(x2, o_dn, o_mla, wzg, wdn, wmla, wo, nw, g, b)


FFN_TM = 512
FFN_SLAB = 256


def _ffn_kernel(h_ref, p_ref, win_ref, wout_ref, wpg_ref, wple_ref, g_ref, b_ref, o_ref):
    h = h_ref[...]
    hb = h.astype(BF16)
    acc = jnp.zeros(h.shape, F32)
    for j in range(FFN_HIDDEN // FFN_SLAB):
        cs = slice(j * FFN_SLAB, (j + 1) * FFN_SLAB)
        gt = _dot(hb, win_ref[:, cs])
        up = _dot(hb, win_ref[:, FFN_HIDDEN + j * FFN_SLAB:FFN_HIDDEN + (j + 1) * FFN_SLAB])
        a = (_silu(gt) * up).astype(BF16)
        acc = acc + _dot(a, wout_ref[cs, :])
    ple = _sigmoid(_dot(hb, wpg_ref[...])) * _dot(p_ref[...].astype(BF16), wple_ref[...])
    r = DEEPNORM_ALPHA * h + acc + ple
    o_ref[...] = _layer_norm(r, g_ref[...], b_ref[...])


def _ffn(h1, p2, win, wout, wpg, wple, g, b):
    assert FFN_HIDDEN % FFN_SLAB == 0
    t = h1.shape[0]
    tm = FFN_TM
    row = lambda i: (i, 0)
    act = pl.BlockSpec((tm, D_MODEL), row)
    return pl.pallas_call(
        _ffn_kernel,
        grid=(t // tm,),
        in_specs=[act, pl.BlockSpec((tm, PLE_DIM), row), _const_spec(win.shape), _const_spec(wout.shape),
                  _const_spec(wpg.shape), _const_spec(wple.shape), _const_spec(g.shape), _const_spec(b.shape)],
        out_specs=act,
        out_shape=jax.ShapeDtypeStruct((t, D_MODEL), F32),
        compiler_params=_params(1),
        name="ffn",
    )(h1, p2, win, wout, wpg, wple, g, b)


def _split_w_in(w_in):
    offs = [0]
    for n in IN_SIZES:
        offs.append(offs[-1] + n)
    return [w_in[:, offs[i]:offs[i + 1]] for i in range(len(IN_SIZES))]


def kernel(x, p, positions, w_in, conv_w, dn_a_log, dn_dt_bias, dn_norm_w, q_norm_w, w_uq, kv_norm_w, w_uk, w_uv,
           w_br_dn, w_br_mla, w_o, ln1_g, ln1_b, w_ffn_in, w_ffn_out, w_ple, w_ple_gate, ln2_g, ln2_b):
    assert w_in.shape[0] == DEPTH
    batch, seq, _ = x.shape
    t = batch * seq
    h = x.reshape(t, D_MODEL)
    pos2 = positions.reshape(t, 1)
    half = MLA_ROPE // 2
    invf = ROPE_BASE ** (-jnp.arange(0, MLA_ROPE, 2, dtype=F32) / MLA_ROPE)
    invf = jnp.tile(invf, LANES // half).reshape(1, LANES)

    for i in range(DEPTH):
        w_qkv, w_z, w_b, w_a, w_cq, w_ckv, w_kr, w_gdn, w_gmla = _split_w_in(w_in[i])
        w_kr_sw = jnp.concatenate([w_kr[:, half:], w_kr[:, :half]], axis=1)
        pad = jnp.zeros((D_MODEL, LAT_W - LAT_BA - 2 * DN_HEADS), F32)
        w_lat = jnp.concatenate([w_cq, w_ckv, w_kr, w_kr_sw, w_b, w_a, pad], axis=1).astype(BF16)
        w_zg = jnp.concatenate([w_z, w_gdn, w_gmla], axis=1).astype(BF16)
        uq = w_uq[i]
        t1 = uq[:, :, MLA_NOPE:MLA_NOPE + half]
        t2 = uq[:, :, MLA_NOPE + half:]
        wqn = uq[:, :, :MLA_NOPE].reshape(MLA_Q_LORA, MLA_HEADS * MLA_NOPE).astype(BF16)
        wqr = jnp.concatenate([t1, t2, t2, t1], axis=2).reshape(MLA_Q_LORA, MLA_HEADS * LANES).astype(BF16)
        wuk = w_uk[i].reshape(MLA_KV_LORA, MLA_HEADS * MLA_NOPE).astype(BF16)
        wuv = w_uv[i].reshape(MLA_KV_LORA, MLA_HEADS * MLA_V).astype(BF16)
        lane_pad = jnp.zeros((LANES - 2 * DN_HEADS,), F32)
        head_pad = jnp.zeros((DN_HEADS,), F32)
        dnp = jnp.stack([jnp.concatenate([head_pad, dn_a_log[i], lane_pad]),
                         jnp.concatenate([head_pad, dn_dt_bias[i], lane_pad])])
        nw = jnp.tile(dn_norm_w[i], 1).reshape(1, DN_DV)
        w_ffn = w_ffn_in[i].astype(BF16)

        q, k, v = _dn_qkv(h, w_qkv.astype(BF16), conv_w[i], seq)
        qn, qr, kn, kr, vm, bg = _latent(h, w_lat, pos2, invf, q_norm_w[i].reshape(1, -1),
                                         kv_norm_w[i].reshape(1, -1), wqn, wqr, wuk, wuv, dnp)
        w, qd, kt, u, intra, egl = _dn_prep(q, k, v, bg)
        o_dn = _dn_scan(w, qd, kt, u, intra, egl, batch, seq)
        o_mla = _mla_attn(qn, qr, kn, kr, vm, batch, seq)
        h1 = _merge(h, o_dn, o_mla, w_zg, w_br_dn[i].astype(BF16), w_br_mla[i].astype(BF16),
                    w_o[i].astype(BF16), nw, ln1_g[i].reshape(1, -1), ln1_b[i].reshape(1, -1))
        h = _ffn(h1, p[i].reshape(t, PLE_DIM), w_ffn,
                 w_ffn_out[i].astype(BF16), w_ple_gate[i].astype(BF16), w_ple[i].astype(BF16),
                 ln2_g[i].reshape(1, -1), ln2_b[i].reshape(1, -1))
    return h.reshape(batch, seq, D_MODEL)
```

```python
import functools
import math

import jax
import jax.numpy as jnp
from jax import lax
from jax.experimental import pallas as pl
from jax.experimental.pallas import tpu as pltpu

D_MODEL = 1024
PLE_DIM = 256
DN_HEADS = 8
DN_DK = 128
DN_DV = 128
DN_CONV = 4
DN_CHUNK = 64
DN_QK_W = DN_HEADS * DN_DK
DN_V_W = DN_HEADS * DN_DV
DN_CONV_CH = 2 * DN_QK_W + DN_V_W
MLA_HEADS = 8
MLA_Q_LORA = 384
MLA_KV_LORA = 256
MLA_NOPE = 128
MLA_ROPE = 64
MLA_V = 128
ROPE_BASE = 10000.0
FFN_HIDDEN = -(-8 * D_MODEL // (3 * 256)) * 256
DEPTH = 1
DEEPNORM_ALPHA = (2.0 * DEPTH) ** 0.25
IN_SIZES = (DN_CONV_CH, DN_V_W, DN_HEADS, DN_HEADS, MLA_Q_LORA, MLA_KV_LORA, MLA_ROPE, D_MODEL, D_MODEL)

LANES = 128
SUBLANES = 8
VMEM_LIMIT_BYTES = 56 * 1024 * 1024

LAT_W = 896
LAT_CKV = MLA_Q_LORA
LAT_KR = LAT_CKV + MLA_KV_LORA
LAT_BA = LAT_KR + 2 * MLA_ROPE

BF16 = jnp.bfloat16
F32 = jnp.float32


def _dot(a, b):
    return jnp.dot(a, b, preferred_element_type=F32)


def _dot_nt(a, b):
    return lax.dot_general(a, b, (((1,), (1,)), ((), ())), preferred_element_type=F32)


def _dot_tn(a, b):
    return lax.dot_general(a, b, (((0,), (0,)), ((), ())), preferred_element_type=F32)


def _sigmoid(x):
    return 0.5 * jnp.tanh(0.5 * x) + 0.5


def _silu(x):
    h = 0.5 * x
    return h * jnp.tanh(h) + h


def _layer_norm(r, g, b):
    mu = jnp.mean(r, axis=-1, keepdims=True)
    c = r - mu
    var = jnp.mean(c * c, axis=-1, keepdims=True)
    return c * lax.rsqrt(var + 1e-5) * g + b


def _const_spec(shape):
    return pl.BlockSpec(shape, lambda *_: (0,) * len(shape))


def _params(n_axes, sem=None):
    return pltpu.CompilerParams(
        dimension_semantics=sem or ("arbitrary",) * n_axes, vmem_limit_bytes=VMEM_LIMIT_BYTES)


QKV_TM = 512
QKV_SLAB = 512
QKV_ROWS = 64


def _dn_qkv_kernel(tiles_per_seq, x_ref, w_ref, cw_ref, q_ref, k_ref, v_ref, pre_ref, tail_ref):
    i = pl.program_id(0)
    tm = x_ref.shape[0]
    assert DN_CONV == 4

    @pl.when(i % tiles_per_seq == 0)
    def _():
        pre_ref[:SUBLANES, :] = jnp.zeros((SUBLANES, DN_CONV_CH), F32)
        tail_ref[...] = jnp.zeros_like(tail_ref)

    xb = x_ref[...].astype(BF16)
    outs = (q_ref, k_ref, v_ref)
    slabs_per_out = DN_QK_W // QKV_SLAB
    for j in range(DN_CONV_CH // QKV_SLAB):
        cs = slice(j * QKV_SLAB, (j + 1) * QKV_SLAB)
        pre_ref[SUBLANES:, cs] = _dot(xb, w_ref[:, cs])
        cw = cw_ref[:, cs]
        out_ref = outs[j // slabs_per_out]
        col0 = (j % slabs_per_out) * QKV_SLAB
        which = j // slabs_per_out
        pair_tail = tail_ref[:, cs]
        for r in range(tm // QKV_ROWS):
            ext = pre_ref[r * QKV_ROWS:(r + 1) * QKV_ROWS + SUBLANES, cs]
            pre = ext[SUBLANES:]
            prev = pltpu.roll(ext, 1, axis=0)[SUBLANES:]
            pair = pre * cw[1:2] + prev * cw[0:1]
            pair2 = pltpu.roll(jnp.concatenate([pair_tail, pair], axis=0), 2, axis=0)[SUBLANES:]
            pair_tail = pair[QKV_ROWS - SUBLANES:]
            y = _silu(pre * cw[3:4] + prev * cw[2:3] + pair2)
            rs = slice(r * QKV_ROWS, (r + 1) * QKV_ROWS)
            for hh in range(QKV_SLAB // DN_DK):
                yh = y[:, hh * DN_DK:(hh + 1) * DN_DK]
                if which < 2:
                    ss = jnp.sum(yh * yh, axis=-1, keepdims=True)
                    scale = lax.rsqrt(ss + 1e-6)
                    yh = yh * (scale * (DN_DK ** -0.5) if which == 0 else scale)
                out_ref[rs, col0 + hh * DN_DK:col0 + (hh + 1) * DN_DK] = yh.astype(out_ref.dtype)
        tail_ref[:, cs] = pair_tail
        pre_ref[:SUBLANES, cs] = pre_ref[tm:, cs]


def _dn_qkv(x2, w_qkv, conv_w, seq):
    t = x2.shape[0]
    tm = QKV_TM
    out = jax.ShapeDtypeStruct((t, DN_QK_W), BF16)
    row = lambda i: (i, 0)
    return pl.pallas_call(
        functools.partial(_dn_qkv_kernel, seq // tm),
        grid=(t // tm,),
        in_specs=[pl.BlockSpec((tm, D_MODEL), row), _const_spec((D_MODEL, DN_CONV_CH)),
                  _const_spec((DN_CONV, DN_CONV_CH))],
        out_specs=[pl.BlockSpec((tm, DN_QK_W), row)] * 3,
        out_shape=[out] * 3,
        scratch_shapes=[pltpu.VMEM((SUBLANES + tm, DN_CONV_CH), F32), pltpu.VMEM((SUBLANES, DN_CONV_CH), F32)],
        compiler_params=_params(1),
        name="dn_qkv",
    )(x2, w_qkv, conv_w)


LAT_TM = 512
MLA_SCALE = (MLA_NOPE + MLA_ROPE) ** -0.5 * math.log2(math.e)


def _rms(t, w):
    return t * lax.rsqrt(jnp.mean(t * t, axis=-1, keepdims=True) + 1e-6) * w


def _latent_kernel(x_ref, wl_ref, pos_ref, invf_ref, qnw_ref, kvw_ref, wqn_ref, wqr_ref, wuk_ref, wuv_ref,
                   dnp_ref, qn_ref, qr_ref, kn_ref, kr_ref, v_ref, bg_ref):
    xb = x_ref[...].astype(BF16)
    lat = _dot(xb, wl_ref[...])

    tm = x_ref.shape[0]
    qt = tm // 4
    half = MLA_ROPE // 2
    lane = lax.broadcasted_iota(jnp.int32, (1, LANES), 1)
    pos = pos_ref[...].astype(F32)
    grp = lane // half
    pos4 = jnp.where(grp == 0, pos[:qt], jnp.where(grp == 1, pos[qt:2 * qt],
                                                   jnp.where(grp == 2, pos[2 * qt:3 * qt], pos[3 * qt:])))
    ang = pos4 * invf_ref[...]
    cos3 = jnp.concatenate(_split3(jnp.cos(ang)), axis=1)
    sin3 = jnp.concatenate(_split3(jnp.sin(ang)), axis=1)
    src = lax.broadcasted_iota(jnp.int32, (3 * LANES, LANES), 0) % LANES
    dst = lax.broadcasted_iota(jnp.int32, (3 * LANES, LANES), 1)
    cos_parts, sin_parts = [], []
    for g in range(4):
        hit = (src // half == g) & (src % half == dst % half) & (dst < MLA_ROPE)
        cos_parts.append(_dot(cos3, jnp.where(hit, 1.0, 0.0).astype(BF16)))
        sin_parts.append(_dot(sin3, jnp.where(hit, jnp.where(dst < half, -1.0, 1.0), 0.0).astype(BF16)))
    cos_t = jnp.concatenate(cos_parts, axis=0)
    sin_t = jnp.concatenate(sin_parts, axis=0)

    def rope(a):
        return a * cos_t + pltpu.roll(a, MLA_ROPE, axis=1) * sin_t

    c_q = _rms(lat[:, :MLA_Q_LORA], qnw_ref[...]).astype(BF16)
    qn_ref[...] = (_dot(c_q, wqn_ref[...]) * MLA_SCALE).astype(qn_ref.dtype)
    qr_raw = _dot(c_q, wqr_ref[...])
    for h in range(MLA_HEADS):
        hs = slice(h * LANES, (h + 1) * LANES)
        qr_ref[:, hs] = (rope(qr_raw[:, hs]) * MLA_SCALE).astype(qr_ref.dtype)

    c_kv = _rms(lat[:, LAT_CKV:LAT_KR], kvw_ref[...]).astype(BF16)
    kn_ref[...] = _dot(c_kv, wuk_ref[...]).astype(kn_ref.dtype)
    v_ref[...] = _dot(c_kv, wuv_ref[...]).astype(v_ref.dtype)
    kr_ref[...] = rope(lat[:, LAT_KR:LAT_BA]).astype(kr_ref.dtype)

    ba = lat[:, LAT_BA:LAT_BA + LANES]
    beta = _sigmoid(ba)
    sp_in = ba + dnp_ref[1:2, :]
    softplus = jnp.maximum(sp_in, 0.0) + jnp.log(1.0 + jnp.exp(-jnp.abs(sp_in)))
    g = -jnp.exp(dnp_ref[0:1, :]) * softplus
    bg_ref[...] = jnp.where(lane < DN_HEADS, beta, jnp.where(lane < 2 * DN_HEADS, g, 0.0))


def _latent(x2, w_lat, pos2, invf, qnw, kvw, wqn, wqr, wuk, wuv, dnp):
    t = x2.shape[0]
    tm = LAT_TM
    row = lambda i: (i, 0)
    wide = jax.ShapeDtypeStruct((t, MLA_HEADS * LANES), BF16)
    return pl.pallas_call(
        _latent_kernel,
        grid=(t // tm,),
        in_specs=[pl.BlockSpec((tm, D_MODEL), row), _const_spec(w_lat.shape), pl.BlockSpec((tm, 1), row),
                  _const_spec(invf.shape), _const_spec(qnw.shape), _const_spec(kvw.shape),
                  _const_spec(wqn.shape), _const_spec(wqr.shape), _const_spec(wuk.shape), _const_spec(wuv.shape),
                  _const_spec(dnp.shape)],
        out_specs=[pl.BlockSpec((tm, MLA_HEADS * LANES), row)] * 3 + [pl.BlockSpec((tm, LANES), row)]
        + [pl.BlockSpec((tm, MLA_HEADS * LANES), row), pl.BlockSpec((tm, LANES), row)],
        out_shape=[wide, wide, wide, jax.ShapeDtypeStruct((t, LANES), BF16), wide,
                   jax.ShapeDtypeStruct((t, LANES), F32)],
        compiler_params=_params(1),
        name="latent",
    )(x2, w_lat, pos2, invf, qnw, kvw, wqn, wqr, wuk, wuv, dnp)


PREP_ROWS = 512
PREP_HEADS = 8
C = DN_CHUNK


def _split3(a):
    a1 = a.astype(BF16)
    r1 = a - a1.astype(F32)
    a2 = r1.astype(BF16)
    a3 = (r1 - a2.astype(F32)).astype(BF16)
    return a1, a2, a3


def _pair_blockdiag(m):
    lane = lax.broadcasted_iota(jnp.int32, m.shape, 1)
    zero = jnp.zeros_like(m)
    return jnp.concatenate([jnp.where(lane < C, m, zero), jnp.where(lane >= C, m, zero)], axis=0)


def _pair_mm(a, b):
    return _dot(a.astype(BF16), _pair_blockdiag(b.astype(BF16)))


def _dn_prep_kernel(q_ref, k_ref, v_ref, bg_ref, w_ref, qd_ref, kt_ref, u_ref, intra_ref, egl_ref):
    assert PREP_HEADS == DN_HEADS
    rows = q_ref.shape[0]
    nc = rows // C
    pc = 2 * C

    row = lax.broadcasted_iota(jnp.int32, (C, pc), 0)
    lane = lax.broadcasted_iota(jnp.int32, (C, pc), 1)
    col = lane % C
    is_a = lane < C
    tril = row >= col
    strict = row > col
    eye = (row == col).astype(F32)
    tril3 = (lax.broadcasted_iota(jnp.int32, (C, 3 * C), 1) % C
             <= lax.broadcasted_iota(jnp.int32, (C, 3 * C), 0)).astype(BF16)

    bg = bg_ref[...]
    betas = [jnp.broadcast_to(bg[:, hh:hh + 1], (rows, LANES)) for hh in range(DN_HEADS)]

    chunks = [slice(c * C, (c + 1) * C) for c in range(nc)]
    units = [(hh, p) for hh in range(PREP_HEADS) for p in range(nc // 2)]
    cums = [_dot(tril3, jnp.concatenate(_split3(bg[rs, :]), axis=0)) for rs in chunks]
    gcs = [[jnp.broadcast_to(cum[:, DN_HEADS + hh:DN_HEADS + hh + 1], (C, LANES)) for cum in cums]
           for hh in range(DN_HEADS)]
    pick_t = (lax.broadcasted_iota(jnp.int32, (2 * SUBLANES, 3 * LANES), 1) % LANES
              == lax.broadcasted_iota(jnp.int32, (2 * SUBLANES, 3 * LANES), 0) + DN_HEADS).astype(BF16)
    cum_t = [_dot_nt(pick_t, jnp.concatenate([jnp.concatenate(_split3(cums[2 * p]), axis=1),
                                              jnp.concatenate(_split3(cums[2 * p + 1]), axis=1)], axis=0))
             for p in range(nc // 2)]
    rowforms = [jnp.broadcast_to(cum_t[p][hh:hh + 1, :], (C, pc)) for hh, p in units]
    decays, kbs, l_mats, qks = [], [], [], []
    for (hh, p), rowform in zip(units, rowforms):
        rs, hs = slice(p * pc, (p + 1) * pc), slice(hh * DN_DK, (hh + 1) * DN_DK)
        diff = jnp.where(is_a, gcs[hh][2 * p], gcs[hh][2 * p + 1]) - rowform
        decay = jnp.where(tril, jnp.exp(jnp.where(tril, diff, 0.0)), 0.0)
        k = k_ref[rs, hs]
        kb = k * betas[hh][rs, :]
        kbb, qb = kb.astype(BF16), q_ref[rs, hs].astype(BF16)
        res = _dot_nt(jnp.concatenate([kbb[:C], qb[:C], kbb[C:], qb[C:]], axis=0), k.astype(BF16))
        decays.append(decay)
        kbs.append(kb)
        l_mats.append(jnp.where(strict, jnp.where(is_a, res[:C], res[2 * C:3 * C]) * decay, 0.0))
        qks.append(jnp.where(is_a, res[C:2 * C], res[3 * C:]))

    invs = [eye] * len(units)
    s = 1
    while s < C:
        blk = ((row // (2 * s)) == (col // (2 * s))) & ((row % (2 * s)) >= s) & ((col % (2 * s)) < s)
        offs = [jnp.where(blk, l, 0.0) for l in l_mats]
        if s == 1:
            invs = [inv - off for inv, off in zip(invs, offs)]
        else:
            mids = [_pair_mm(off, inv) for off, inv in zip(offs, invs)]
            invs = [inv - _pair_mm(inv, mid) for inv, mid in zip(invs, mids)]
        s *= 2

    for n, (hh, p) in enumerate(units):
        rs, hs = slice(p * pc, (p + 1) * pc), slice(hh * DN_DK, (hh + 1) * DN_DK)
        ga, gb = gcs[hh][2 * p], gcs[hh][2 * p + 1]
        egc = jnp.exp(jnp.concatenate([ga, gb], axis=0))
        rhs = jnp.concatenate([(v_ref[rs, hs] * betas[hh][rs, :]).astype(BF16), (kbs[n] * egc).astype(BF16)], axis=1)
        uw = _dot(_pair_blockdiag(invs[n].astype(BF16)), rhs)
        u_ref[rs, hs] = uw[:, :DN_DV].astype(u_ref.dtype)
        w_ref[rs, hs] = uw[:, DN_DV:].astype(w_ref.dtype)
        intra_ref[hh, p * C:(p + 1) * C, :] = jnp.where(tril, qks[n] * decays[n], 0.0).astype(intra_ref.dtype)
        qd_ref[rs, hs] = (q_ref[rs, hs] * egc).astype(qd_ref.dtype)
        gla, glb = ga[C - 1:C, :], gb[C - 1:C, :]
        tail = jnp.exp(jnp.concatenate([gla - ga, glb - gb], axis=0))
        kt_ref[rs, hs] = (k_ref[rs, hs] * tail).astype(kt_ref.dtype)
        egl_ref[hh, 2 * p:2 * p + 1, :] = jnp.exp(gla)
        egl_ref[hh, 2 * p + 1:2 * p + 2, :] = jnp.exp(glb)


def _dn_prep(q, k, v, bg):
    t = q.shape[0]
    rows = PREP_ROWS
    blk = pl.BlockSpec((rows, PREP_HEADS * DN_DK), lambda i, h: (i, h))
    wide_bf = jax.ShapeDtypeStruct((t, DN_QK_W), BF16)
    return pl.pallas_call(
        _dn_prep_kernel,
        grid=(t // rows, DN_HEADS // PREP_HEADS),
        in_specs=[blk, blk, blk, pl.BlockSpec((rows, LANES), lambda i, h: (i, 0))],
        out_specs=[blk, blk, blk, blk, pl.BlockSpec((PREP_HEADS, rows // 2, 2 * C), lambda i, h: (h, i, 0)),
                   pl.BlockSpec((PREP_HEADS, rows // C, LANES), lambda i, h: (h, i, 0))],
        out_shape=[wide_bf, wide_bf, wide_bf, wide_bf,
                   jax.ShapeDtypeStruct((DN_HEADS, t // 2, 2 * C), BF16),
                   jax.ShapeDtypeStruct((DN_HEADS, t // C, LANES), F32)],
        compiler_params=_params(2),
        name="dn_prep",
    )(q, k, v, bg)


SCAN_ROWS = 512
SCAN_BATCH = 2


def _dn_scan_kernel(w_ref, qd_ref, kt_ref, u_ref, intra_ref, egl_ref, o_ref, state_ref):
    @pl.when(pl.program_id(1) == 0)
    def _():
        state_ref[...] = jnp.zeros_like(state_ref)

    nb, rows = w_ref.shape[0], w_ref.shape[1]
    chains = [(bb, h, slice(h * DN_DK, (h + 1) * DN_DK)) for bb in range(nb) for h in range(DN_HEADS)]
    states = [state_ref[bb, h] for bb, h, _ in chains]
    zeros = jnp.zeros((C, DN_DV), BF16)
    for c in range(rows // C):
        rs = slice(c * C, (c + 1) * C)
        r1s = [_dot(jnp.concatenate([w_ref[bb, rs, hs], qd_ref[bb, rs, hs]], axis=0), st.astype(BF16))
               for (bb, _, hs), st in zip(chains, states)]
        vbs = [(u_ref[bb, rs, hs] - r1[:C]).astype(BF16) for (bb, _, hs), r1 in zip(chains, r1s)]
        ps = slice((c // 2) * C, (c // 2 + 1) * C)
        vps = [jnp.concatenate([vb, zeros] if c % 2 == 0 else [zeros, vb], axis=0) for vb in vbs]
        outs = [r1[C:] + _dot(intra_ref[h, bb, ps, :], vp) for (bb, h, _), r1, vp in zip(chains, r1s, vps)]
        upds = [_dot_tn(kt_ref[bb, rs, hs], vb) for (bb, _, hs), vb in zip(chains, vbs)]
        for (bb, _, hs), out in zip(chains, outs):
            o_ref[bb, rs, hs] = out.astype(o_ref.dtype)
        states = [st * egl_ref[h, bb, c:c + 1, :] + upd for (bb, h, _), st, upd in zip(chains, states, upds)]
    for (bb, h, _), st in zip(chains, states):
        state_ref[bb, h] = st


def _dn_scan(w, qd, kt, u, intra, egl, batch, seq):
    rows = SCAN_ROWS
    nb = SCAN_BATCH
    as_bs = lambda a: a.reshape(batch, seq, a.shape[-1])
    wide = pl.BlockSpec((nb, rows, DN_QK_W), lambda b, i: (b, i, 0))
    intra4 = intra.reshape(DN_HEADS, batch, seq // 2, 2 * C)
    egl4 = egl.reshape(DN_HEADS, batch, seq // C, LANES)
    out = pl.pallas_call(
        _dn_scan_kernel,
        grid=(batch // nb, seq // rows),
        in_specs=[wide, wide, wide, wide,
                  pl.BlockSpec((DN_HEADS, nb, rows // 2, 2 * C), lambda b, i: (0, b, i, 0)),
                  pl.BlockSpec((DN_HEADS, nb, rows // C, LANES), lambda b, i: (0, b, i, 0))],
        out_specs=wide,
        out_shape=jax.ShapeDtypeStruct((batch, seq, DN_V_W), BF16),
        scratch_shapes=[pltpu.VMEM((nb, DN_HEADS, DN_DK, DN_DV), F32)],
        compiler_params=_params(2),
        name="dn_scan",
    )(as_bs(w), as_bs(qd), as_bs(kt), as_bs(u), intra4, egl4)
    return out.reshape(batch * seq, DN_V_W)


ATT_TQ = 512
ATT_TK = 512
NEG_INF = -1e30


def _mla_attn_kernel(qn_ref, qr_ref, kn_ref, kr_ref, v_ref, o_ref, qcat_ref, kcat_ref, vcat_ref):
    seq = qn_ref.shape[0]
    tq, tk = ATT_TQ, ATT_TK
    qcat_ref[:, :LANES] = qn_ref[...]
    qcat_ref[:, LANES:] = qr_ref[...]
    kcat_ref[:, :LANES] = kn_ref[...]
    kcat_ref[:, LANES:] = kr_ref[...]
    vcat_ref[:, :LANES] = v_ref[...]
    vcat_ref[:, LANES:] = jnp.ones((seq, LANES), BF16)
    causal = (lax.broadcasted_iota(jnp.int32, (tq, tk), 1) <= lax.broadcasted_iota(jnp.int32, (tq, tk), 0))

    for i in range(seq // tq):
        q = qcat_ref[i * tq:(i + 1) * tq, :]
        m = acc = None
        for j in range(i + 1):
            ks = slice(j * tk, (j + 1) * tk)
            s = _dot_nt(q, kcat_ref[ks, :])
            if j == i:
                s = jnp.where(causal, s, NEG_INF)
            m_cur = jnp.max(s, axis=1, keepdims=True)
            if j == 0:
                m_new = jnp.broadcast_to(m_cur, (tq, LANES))
            else:
                m_new = jnp.maximum(m, m_cur)
            p = jnp.exp2(s - jnp.concatenate([m_new] * (tk // LANES), axis=1))
            pv = _dot(p.astype(BF16), vcat_ref[ks, :])
            if j == 0:
                acc = pv
            else:
                alpha = jnp.exp2(m - m_new)
                acc = jnp.concatenate([alpha, alpha], axis=1) * acc + pv
            m = m_new
        o_ref[i * tq:(i + 1) * tq, :] = (acc[:, :LANES] / acc[:, LANES:]).astype(o_ref.dtype)


def _mla_attn(qn, qr, kn, kr, v, batch, seq):
    t = qn.shape[0]
    blk = pl.BlockSpec((seq, LANES), lambda b, h: (b, h))
    return pl.pallas_call(
        _mla_attn_kernel,
        grid=(batch, MLA_HEADS),
        in_specs=[blk, blk, blk, pl.BlockSpec((seq, LANES), lambda b, h: (b, 0)), blk],
        out_specs=blk,
        out_shape=jax.ShapeDtypeStruct((t, MLA_HEADS * MLA_V), BF16),
        scratch_shapes=[pltpu.VMEM((seq, 2 * LANES), BF16)] * 3,
        compiler_params=_params(2),
        name="mla_attn",
    )(qn, qr, kn, kr, v)


MERGE_TM = 512


def _merge_kernel(x_ref, odn_ref, omla_ref, wzg_ref, wdn_ref, wmla_ref, wo_ref, nw_ref, g_ref, b_ref, h_ref):
    x = x_ref[...]
    zg = _dot(x.astype(BF16), wzg_ref[...])
    z = zg[:, :DN_V_W]
    parts = []
    for h in range(DN_HEADS):
        hs = slice(h * DN_DV, (h + 1) * DN_DV)
        parts.append(_rms(odn_ref[:, hs].astype(F32), nw_ref[...]))
    o_dn = jnp.concatenate(parts, axis=1) * _silu(z)
    y_dn = _dot(o_dn.astype(BF16), wdn_ref[...])
    y_mla = _dot(omla_ref[...], wmla_ref[...])
    mixed = _sigmoid(zg[:, DN_V_W:DN_V_W + D_MODEL]) * y_dn + _sigmoid(zg[:, DN_V_W + D_MODEL:]) * y_mla
    r = DEEPNORM_ALPHA * x + _dot(mixed.astype(BF16), wo_ref[...])
    h_ref[...] = _layer_norm(r, g_ref[...], b_ref[...])


def _merge(x2, o_dn, o_mla, wzg, wdn, wmla, wo, nw, g, b):
    t = x2.shape[0]
    tm = MERGE_TM
    row = lambda i: (i, 0)
    act = pl.BlockSpec((tm, D_MODEL), row)
    return pl.pallas_call(
        _merge_kernel,
        grid=(t // tm,),
        in_specs=[act, act, act, _const_spec(wzg.shape), _const_spec(wdn.shape), _const_spec(wmla.shape),
                  _const_spec(wo.shape), _const_spec(nw.shape), _const_spec(g.shape), _const_spec(b.shape)],
        out_specs=act,
        out_shape=jax.ShapeDtypeStruct((t, D_MODEL), F32),
        compiler_params=_params(1),
        name="merge",
    )(x2, o_dn, o_mla, wzg, wdn, wmla, wo, nw, g, b)


FFN_TM = 512
FFN_SLAB = 256


def _ffn_kernel(h_ref, p_ref, win_ref, wout_ref, wpg_ref, wple_ref, g_ref, b_ref, o_ref):
    h = h_ref[...]
    hb = h.astype(BF16)
    acc = jnp.zeros(h.shape, F32)
    for j in range(FFN_HIDDEN // FFN_SLAB):
        cs = slice(j * FFN_SLAB, (j + 1) * FFN_SLAB)
        gt = _dot(hb, win_ref[:, cs])
        up = _dot(hb, win_ref[:, FFN_HIDDEN + j * FFN_SLAB:FFN_HIDDEN + (j + 1) * FFN_SLAB])
        a = (_silu(gt) * up).astype(BF16)
        acc = acc + _dot(a, wout_ref[cs, :])
    ple = _sigmoid(_dot(hb, wpg_ref[...])) * _dot(p_ref[...].astype(BF16), wple_ref[...])
    r = DEEPNORM_ALPHA * h + acc + ple
    o_ref[...] = _layer_norm(r, g_ref[...], b_ref[...])


def _ffn(h1, p2, win, wout, wpg, wple, g, b):
    assert FFN_HIDDEN % FFN_SLAB == 0
    t = h1.shape[0]
    tm = FFN_TM
    row = lambda i: (i, 0)
    act = pl.BlockSpec((tm, D_MODEL), row)
    return pl.pallas_call(
        _ffn_kernel,
        grid=(t // tm,),
        in_specs=[act, pl.BlockSpec((tm, PLE_DIM), row), _const_spec(win.shape), _const_spec(wout.shape),
                  _const_spec(wpg.shape), _const_spec(wple.shape), _const_spec(g.shape), _const_spec(b.shape)],
        out_specs=act,
        out_shape=jax.ShapeDtypeStruct((t, D_MODEL), F32),
        compiler_params=_params(1),
        name="ffn",
    )(h1, p2, win, wout, wpg, wple, g, b)


def _split_w_in(w_in):
    offs = [0]
    for n in IN_SIZES:
        offs.append(offs[-1] + n)
    return [w_in[:, offs[i]:offs[i + 1]] for i in range(len(IN_SIZES))]


def kernel(x, p, positions, w_in, conv_w, dn_a_log, dn_dt_bias, dn_norm_w, q_norm_w, w_uq, kv_norm_w, w_uk, w_uv,
           w_br_dn, w_br_mla, w_o, ln1_g, ln1_b, w_ffn_in, w_ffn_out, w_ple, w_ple_gate, ln2_g, ln2_b):
    assert w_in.shape[0] == DEPTH
    batch, seq, _ = x.shape
    t = batch * seq
    h = x.reshape(t, D_MODEL)
    pos2 = positions.reshape(t, 1)
    half = MLA_ROPE // 2
    invf = ROPE_BASE ** (-jnp.arange(0, MLA_ROPE, 2, dtype=F32) / MLA_ROPE)
    invf = jnp.tile(invf, LANES // half).reshape(1, LANES)

    for i in range(DEPTH):
        w_qkv, w_z, w_b, w_a, w_cq, w_ckv, w_kr, w_gdn, w_gmla = _split_w_in(w_in[i])
        w_kr_sw = jnp.concatenate([w_kr[:, half:], w_kr[:, :half]], axis=1)
        pad = jnp.zeros((D_MODEL, LAT_W - LAT_BA - 2 * DN_HEADS), F32)
        w_lat = jnp.concatenate([w_cq, w_ckv, w_kr, w_kr_sw, w_b, w_a, pad], axis=1).astype(BF16)
        w_zg = jnp.concatenate([w_z, w_gdn, w_gmla], axis=1).astype(BF16)
        uq = w_uq[i]
        t1 = uq[:, :, MLA_NOPE:MLA_NOPE + half]
        t2 = uq[:, :, MLA_NOPE + half:]
        wqn = uq[:, :, :MLA_NOPE].reshape(MLA_Q_LORA, MLA_HEADS * MLA_NOPE).astype(BF16)
        wqr = jnp.concatenate([t1, t2, t2, t1], axis=2).reshape(MLA_Q_LORA, MLA_HEADS * LANES).astype(BF16)
        wuk = w_uk[i].reshape(MLA_KV_LORA, MLA_HEADS * MLA_NOPE).astype(BF16)
        wuv = w_uv[i].reshape(MLA_KV_LORA, MLA_HEADS * MLA_V).astype(BF16)
        lane_pad = jnp.zeros((LANES - 2 * DN_HEADS,), F32)
        head_pad = jnp.zeros((DN_HEADS,), F32)
        dnp = jnp.stack([jnp.concatenate([head_pad, dn_a_log[i], lane_pad]),
                         jnp.concatenate([head_pad, dn_dt_bias[i], lane_pad])])
        nw = jnp.tile(dn_norm_w[i], 1).reshape(1, DN_DV)
        w_ffn = w_ffn_in[i].astype(BF16)

        q, k, v = _dn_qkv(h, w_qkv.astype(BF16), conv_w[i], seq)
        qn, qr, kn, kr, vm, bg = _latent(h, w_lat, pos2, invf, q_norm_w[i].reshape(1, -1),
                                         kv_norm_w[i].reshape(1, -1), wqn, wqr, wuk, wuv, dnp)
        w, qd, kt, u, intra, egl = _dn_prep(q, k, v, bg)
        o_dn = _dn_scan(w, qd, kt, u, intra, egl, batch, seq)
        o_mla = _mla_attn(qn, qr, kn, kr, vm, batch, seq)
        h1 = _merge(h, o_dn, o_mla, w_zg, w_br_dn[i].astype(BF16), w_br_mla[i].astype(BF16),
                    w_o[i].astype(BF16), nw, ln1_g[i].reshape(1, -1), ln1_b[i].reshape(1, -1))
        h = _ffn(h1, p[i].reshape(t, PLE_DIM), w_ffn,
                 w_ffn_out[i].astype(BF16), w_ple_gate[i].astype(BF16), w_ple[i].astype(BF16),
                 ln2_g[i].reshape(1, -1), ln2_b[i].reshape(1, -1))
    return h.reshape(batch, seq, D_MODEL)
```

```python
import functools
import math

import jax
import jax.numpy as jnp
from jax import lax
from jax.experimental import pallas as pl
from jax.experimental.pallas import tpu as pltpu

D_MODEL = 1024
PLE_DIM = 256
DN_HEADS = 8
DN_DK = 128
DN_DV = 128
DN_CONV = 4
DN_CHUNK = 64
DN_QK_W = DN_HEADS * DN_DK
DN_V_W = DN_HEADS * DN_DV
DN_CONV_CH = 2 * DN_QK_W + DN_V_W
MLA_HEADS = 8
MLA_Q_LORA = 384
MLA_KV_LORA = 256
MLA_NOPE = 128
MLA_ROPE = 64
MLA_V = 128
ROPE_BASE = 10000.0
FFN_HIDDEN = -(-8 * D_MODEL // (3 * 256)) * 256
DEPTH = 1
DEEPNORM_ALPHA = (2.0 * DEPTH) ** 0.25
IN_SIZES = (DN_CONV_CH, DN_V_W, DN_HEADS, DN_HEADS, MLA_Q_LORA, MLA_KV_LORA, MLA_ROPE, D_MODEL, D_MODEL)

LANES = 128
SUBLANES = 8
VMEM_LIMIT_BYTES = 56 * 1024 * 1024

LAT_W = 896
LAT_CKV = MLA_Q_LORA
LAT_KR = LAT_CKV + MLA_KV_LORA
LAT_BA = LAT_KR + 2 * MLA_ROPE

BF16 = jnp.bfloat16
F32 = jnp.float32


def _dot(a, b):
    return jnp.dot(a, b, preferred_element_type=F32)


def _dot_nt(a, b):
    return lax.dot_general(a, b, (((1,), (1,)), ((), ())), preferred_element_type=F32)


def _dot_tn(a, b):
    return lax.dot_general(a, b, (((0,), (0,)), ((), ())), preferred_element_type=F32)


def _sigmoid(x):
    return 0.5 * jnp.tanh(0.5 * x) + 0.5


def _silu(x):
    h = 0.5 * x
    return h * jnp.tanh(h) + h


def _layer_norm(r, g, b):
    mu = jnp.mean(r, axis=-1, keepdims=True)
    c = r - mu
    var = jnp.mean(c * c, axis=-1, keepdims=True)
    return c * lax.rsqrt(var + 1e-5) * g + b


def _const_spec(shape):
    return pl.BlockSpec(shape, lambda *_: (0,) * len(shape))


def _params(n_axes, sem=None):
    return pltpu.CompilerParams(
        dimension_semantics=sem or ("arbitrary",) * n_axes, vmem_limit_bytes=VMEM_LIMIT_BYTES)


QKV_TM = 512
QKV_SLAB = 512
QKV_ROWS = 64


def _dn_qkv_kernel(tiles_per_seq, x_ref, w_ref, cw_ref, q_ref, k_ref, v_ref, pre_ref, tail_ref):
    i = pl.program_id(0)
    tm = x_ref.shape[0]
    assert DN_CONV == 4

    @pl.when(i % tiles_per_seq == 0)
    def _():
        pre_ref[:SUBLANES, :] = jnp.zeros((SUBLANES, DN_CONV_CH), F32)
        tail_ref[...] = jnp.zeros_like(tail_ref)

    xb = x_ref[...].astype(BF16)
    outs = (q_ref, k_ref, v_ref)
    slabs_per_out = DN_QK_W // QKV_SLAB
    for j in range(DN_CONV_CH // QKV_SLAB):
        cs = slice(j * QKV_SLAB, (j + 1) * QKV_SLAB)
        pre_ref[SUBLANES:, cs] = _dot(xb, w_ref[:, cs])
        cw = cw_ref[:, cs]
        out_ref = outs[j // slabs_per_out]
        col0 = (j % slabs_per_out) * QKV_SLAB
        which = j // slabs_per_out
        pair_tail = tail_ref[:, cs]
        for r in range(tm // QKV_ROWS):
            ext = pre_ref[r * QKV_ROWS:(r + 1) * QKV_ROWS + SUBLANES, cs]
            pre = ext[SUBLANES:]
            prev = pltpu.roll(ext, 1, axis=0)[SUBLANES:]
            pair = pre * cw[1:2] + prev * cw[0:1]
            pair2 = pltpu.roll(jnp.concatenate([pair_tail, pair], axis=0), 2, axis=0)[SUBLANES:]
            pair_tail = pair[QKV_ROWS - SUBLANES:]
            y = _silu(pre * cw[3:4] + prev * cw[2:3] + pair2)
            rs = slice(r * QKV_ROWS, (r + 1) * QKV_ROWS)
            for hh in range(QKV_SLAB // DN_DK):
                yh = y[:, hh * DN_DK:(hh + 1) * DN_DK]
                if which < 2:
                    ss = jnp.sum(yh * yh, axis=-1, keepdims=True)
                    scale = lax.rsqrt(ss + 1e-6)
                    yh = yh * (scale * (DN_DK ** -0.5) if which == 0 else scale)
                out_ref[rs, col0 + hh * DN_DK:col0 + (hh + 1) * DN_DK] = yh.astype(out_ref.dtype)
        tail_ref[:, cs] = pair_tail
        pre_ref[:SUBLANES, cs] = pre_ref[tm:, cs]


LAT_TM = 512
MLA_SCALE = (MLA_NOPE + MLA_ROPE) ** -0.5 * math.log2(math.e)


def _rms(t, w):
    return t * lax.rsqrt(jnp.mean(t * t, axis=-1, keepdims=True) + 1e-6) * w


def _latent_kernel(x_ref, wl_ref, pos_ref, invf_ref, qnw_ref, kvw_ref, wqn_ref, wqr_ref, wuk_ref, wuv_ref,
                   dnp_ref, qn_ref, qr_ref, kn_ref, kr_ref, v_ref, bg_ref):
    xb = x_ref[...].astype(BF16)
    lat = _dot(xb, wl_ref[...])

    tm = x_ref.shape[0]
    qt = tm // 4
    half = MLA_ROPE // 2
    lane = lax.broadcasted_iota(jnp.int32, (1, LANES), 1)
    pos = pos_ref[...].astype(F32)
    grp = lane // half
    pos4 = jnp.where(grp == 0, pos[:qt], jnp.where(grp == 1, pos[qt:2 * qt],
                                                   jnp.where(grp == 2, pos[2 * qt:3 * qt], pos[3 * qt:])))
    ang = pos4 * invf_ref[...]
    cos3 = jnp.concatenate(_split3(jnp.cos(ang)), axis=1)
    sin3 = jnp.concatenate(_split3(jnp.sin(ang)), axis=1)
    src = lax.broadcasted_iota(jnp.int32, (3 * LANES, LANES), 0) % LANES
    dst = lax.broadcasted_iota(jnp.int32, (3 * LANES, LANES), 1)
    cos_parts, sin_parts = [], []
    for g in range(4):
        hit = (src // half == g) & (src % half == dst % half) & (dst < MLA_ROPE)
        cos_parts.append(_dot(cos3, jnp.where(hit, 1.0, 0.0).astype(BF16)))
        sin_parts.append(_dot(sin3, jnp.where(hit, jnp.where(dst < half, -1.0, 1.0), 0.0).astype(BF16)))
    cos_t = jnp.concatenate(cos_parts, axis=0)
    sin_t = jnp.concatenate(sin_parts, axis=0)

    def rope(a):
        return a * cos_t + pltpu.roll(a, MLA_ROPE, axis=1) * sin_t

    c_q = _rms(lat[:, :MLA_Q_LORA], qnw_ref[...]).astype(BF16)
    qn_ref[...] = (_dot(c_q, wqn_ref[...]) * MLA_SCALE).astype(qn_ref.dtype)
    qr_raw = _dot(c_q, wqr_ref[...])
    for h in range(MLA_HEADS):
        hs = slice(h * LANES, (h + 1) * LANES)
        qr_ref[:, hs] = (rope(qr_raw[:, hs]) * MLA_SCALE).astype(qr_ref.dtype)

    c_kv = _rms(lat[:, LAT_CKV:LAT_KR], kvw_ref[...]).astype(BF16)
    kn_ref[...] = _dot(c_kv, wuk_ref[...]).astype(kn_ref.dtype)
    v_ref[...] = _dot(c_kv, wuv_ref[...]).astype(v_ref.dtype)
    kr_ref[...] = rope(lat[:, LAT_KR:LAT_BA]).astype(kr_ref.dtype)

    ba = lat[:, LAT_BA:LAT_BA + LANES]
    beta = _sigmoid(ba)
    sp_in = ba + dnp_ref[1:2, :]
    softplus = jnp.maximum(sp_in, 0.0) + jnp.log(1.0 + jnp.exp(-jnp.abs(sp_in)))
    g = -jnp.exp(dnp_ref[0:1, :]) * softplus
    bg_ref[...] = jnp.where(lane < DN_HEADS, beta, jnp.where(lane < 2 * DN_HEADS, g, 0.0))


def _input_kernel(tiles_per_seq, x_ref, wqkv_ref, cw_ref, wl_ref, pos_ref, invf_ref, qnw_ref, kvw_ref, wqn_ref,
                  wqr_ref, wuk_ref, wuv_ref, dnp_ref, q_ref, k_ref, v_ref, qn_ref, qr_ref, kn_ref, kr_ref, vm_ref,
                  bg_ref, pre_ref, tail_ref):
    _dn_qkv_kernel(tiles_per_seq, x_ref, wqkv_ref, cw_ref, q_ref, k_ref, v_ref, pre_ref, tail_ref)
    _latent_kernel(x_ref, wl_ref, pos_ref, invf_ref, qnw_ref, kvw_ref, wqn_ref, wqr_ref, wuk_ref, wuv_ref, dnp_ref,
                   qn_ref, qr_ref, kn_ref, kr_ref, vm_ref, bg_ref)


def _input_stage(x2, w_in_bf, conv_w, w_lat, pos2, invf, qnw, kvw, wqn, wqr, wuk, wuv, dnp, seq):
    t = x2.shape[0]
    tm = QKV_TM
    row = lambda i: (i, 0)
    dn_out = jax.ShapeDtypeStruct((t, DN_QK_W), BF16)
    wide = jax.ShapeDtypeStruct((t, MLA_HEADS * LANES), BF16)
    wide_spec = pl.BlockSpec((tm, MLA_HEADS * LANES), row)
    lane_spec = pl.BlockSpec((tm, LANES), row)
    consts = [conv_w, w_lat]
    consts2 = [invf, qnw, kvw, wqn, wqr, wuk, wuv, dnp]
    return pl.pallas_call(
        functools.partial(_input_kernel, seq // tm),
        grid=(t // tm,),
        in_specs=[pl.BlockSpec((tm, D_MODEL), row), _const_spec((D_MODEL, DN_CONV_CH))]
        + [_const_spec(a.shape) for a in consts] + [pl.BlockSpec((tm, 1), row)]
        + [_const_spec(a.shape) for a in consts2],
        out_specs=[pl.BlockSpec((tm, DN_QK_W), row)] * 3 + [wide_spec] * 3 + [lane_spec, wide_spec, lane_spec],
        out_shape=[dn_out] * 3 + [wide, wide, wide, jax.ShapeDtypeStruct((t, LANES), BF16), wide,
                                  jax.ShapeDtypeStruct((t, LANES), F32)],
        scratch_shapes=[pltpu.VMEM((SUBLANES + tm, DN_CONV_CH), F32), pltpu.VMEM((SUBLANES, DN_CONV_CH), F32)],
        compiler_params=_params(1),
        name="input_stage",
    )(x2, w_in_bf, conv_w, w_lat, pos2, invf, qnw, kvw, wqn, wqr, wuk, wuv, dnp)


PREP_ROWS = 512
PREP_HEADS = 8
C = DN_CHUNK


def _split3(a):
    a1 = a.astype(BF16)
    r1 = a - a1.astype(F32)
    a2 = r1.astype(BF16)
    a3 = (r1 - a2.astype(F32)).astype(BF16)
    return a1, a2, a3


def _pair_blockdiag(m):
    lane = lax.broadcasted_iota(jnp.int32, m.shape, 1)
    zero = jnp.zeros_like(m)
    return jnp.concatenate([jnp.where(lane < C, m, zero), jnp.where(lane >= C, m, zero)], axis=0)


def _pair_mm(a, b):
    return _dot(a.astype(BF16), _pair_blockdiag(b.astype(BF16)))


def _dn_prep_kernel(q_ref, k_ref, v_ref, bg_ref, w_ref, qd_ref, kt_ref, u_ref, intra_ref, egl_ref):
    assert PREP_HEADS == DN_HEADS
    rows = q_ref.shape[0]
    nc = rows // C
    pc = 2 * C

    row = lax.broadcasted_iota(jnp.int32, (C, pc), 0)
    lane = lax.broadcasted_iota(jnp.int32, (C, pc), 1)
    col = lane % C
    is_a = lane < C
    tril = row >= col
    strict = row > col
    eye = (row == col).astype(F32)
    tril3 = (lax.broadcasted_iota(jnp.int32, (C, 3 * C), 1) % C
             <= lax.broadcasted_iota(jnp.int32, (C, 3 * C), 0)).astype(BF16)

    bg = bg_ref[...]
    betas = [jnp.broadcast_to(bg[:, hh:hh + 1], (rows, LANES)) for hh in range(DN_HEADS)]

    chunks = [slice(c * C, (c + 1) * C) for c in range(nc)]
    units = [(hh, p) for hh in range(PREP_HEADS) for p in range(nc // 2)]
    cums = [_dot(tril3, jnp.concatenate(_split3(bg[rs, :]), axis=0)) for rs in chunks]
    gcs = [[jnp.broadcast_to(cum[:, DN_HEADS + hh:DN_HEADS + hh + 1], (C, LANES)) for cum in cums]
           for hh in range(DN_HEADS)]
    pick_t = (lax.broadcasted_iota(jnp.int32, (2 * SUBLANES, 3 * LANES), 1) % LANES
              == lax.broadcasted_iota(jnp.int32, (2 * SUBLANES, 3 * LANES), 0) + DN_HEADS).astype(BF16)
    cum_t = [_dot_nt(pick_t, jnp.concatenate([jnp.concatenate(_split3(cums[2 * p]), axis=1),
                                              jnp.concatenate(_split3(cums[2 * p + 1]), axis=1)], axis=0))
             for p in range(nc // 2)]
    rowforms = [jnp.broadcast_to(cum_t[p][hh:hh + 1, :], (C, pc)) for hh, p in units]
    decays, kbs, l_mats, qks = [], [], [], []
    for (hh, p), rowform in zip(units, rowforms):
        rs, hs = slice(p * pc, (p + 1) * pc), slice(hh * DN_DK, (hh + 1) * DN_DK)
        diff = jnp.where(is_a, gcs[hh][2 * p], gcs[hh][2 * p + 1]) - rowform
        decay = jnp.where(tril, jnp.exp(jnp.where(tril, diff, 0.0)), 0.0)
        k = k_ref[rs, hs]
        kb = k * betas[hh][rs, :]
        kbb, qb = kb.astype(BF16), q_ref[rs, hs].astype(BF16)
        res = _dot_nt(jnp.concatenate([kbb[:C], qb[:C], kbb[C:], qb[C:]], axis=0), k.astype(BF16))
        decays.append(decay)
        kbs.append(kb)
        l_mats.append(jnp.where(strict, jnp.where(is_a, res[:C], res[2 * C:3 * C]) * decay, 0.0))
        qks.append(jnp.where(is_a, res[C:2 * C], res[3 * C:]))

    invs = [eye] * len(units)
    s = 1
    while s < C:
        blk = ((row // (2 * s)) == (col // (2 * s))) & ((row % (2 * s)) >= s) & ((col % (2 * s)) < s)
        offs = [jnp.where(blk, l, 0.0) for l in l_mats]
        if s == 1:
            invs = [inv - off for inv, off in zip(invs, offs)]
        else:
            mids = [_pair_mm(off, inv) for off, inv in zip(offs, invs)]
            invs = [inv - _pair_mm(inv, mid) for inv, mid in zip(invs, mids)]
        s *= 2

    for n, (hh, p) in enumerate(units):
        rs, hs = slice(p * pc, (p + 1) * pc), slice(hh * DN_DK, (hh + 1) * DN_DK)
        ga, gb = gcs[hh][2 * p], gcs[hh][2 * p + 1]
        egc = jnp.exp(jnp.concatenate([ga, gb], axis=0))
        rhs = jnp.concatenate([(v_ref[rs, hs] * betas[hh][rs, :]).astype(BF16), (kbs[n] * egc).astype(BF16)], axis=1)
        uw = _dot(_pair_blockdiag(invs[n].astype(BF16)), rhs)
        u_ref[rs, hs] = uw[:, :DN_DV].astype(u_ref.dtype)
        w_ref[rs, hs] = uw[:, DN_DV:].astype(w_ref.dtype)
        intra_ref[hh, p * C:(p + 1) * C, :] = jnp.where(tril, qks[n] * decays[n], 0.0).astype(intra_ref.dtype)
        qd_ref[rs, hs] = (q_ref[rs, hs] * egc).astype(qd_ref.dtype)
        gla, glb = ga[C - 1:C, :], gb[C - 1:C, :]
        tail = jnp.exp(jnp.concatenate([gla - ga, glb - gb], axis=0))
        kt_ref[rs, hs] = (k_ref[rs, hs] * tail).astype(kt_ref.dtype)
        egl_ref[hh, 2 * p:2 * p + 1, :] = jnp.exp(gla)
        egl_ref[hh, 2 * p + 1:2 * p + 2, :] = jnp.exp(glb)


def _dn_prep(q, k, v, bg):
    t = q.shape[0]
    rows = PREP_ROWS
    blk = pl.BlockSpec((rows, PREP_HEADS * DN_DK), lambda i, h: (i, h))
    wide_bf = jax.ShapeDtypeStruct((t, DN_QK_W), BF16)
    return pl.pallas_call(
        _dn_prep_kernel,
        grid=(t // rows, DN_HEADS // PREP_HEADS),
        in_specs=[blk, blk, blk, pl.BlockSpec((rows, LANES), lambda i, h: (i, 0))],
        out_specs=[blk, blk, blk, blk, pl.BlockSpec((PREP_HEADS, rows // 2, 2 * C), lambda i, h: (h, i, 0)),
                   pl.BlockSpec((PREP_HEADS, rows // C, LANES), lambda i, h: (h, i, 0))],
        out_shape=[wide_bf, wide_bf, wide_bf, wide_bf,
                   jax.ShapeDtypeStruct((DN_HEADS, t // 2, 2 * C), BF16),
                   jax.ShapeDtypeStruct((DN_HEADS, t // C, LANES), F32)],
        compiler_params=_params(2),
        name="dn_prep",
    )(q, k, v, bg)


SCAN_ROWS = 512
SCAN_BATCH = 2


def _dn_scan_kernel(w_ref, qd_ref, kt_ref, u_ref, intra_ref, egl_ref, o_ref, state_ref):
    @pl.when(pl.program_id(1) == 0)
    def _():
        state_ref[...] = jnp.zeros_like(state_ref)

    nb, rows = w_ref.shape[0], w_ref.shape[1]
    chains = [(bb, h, slice(h * DN_DK, (h + 1) * DN_DK)) for bb in range(nb) for h in range(DN_HEADS)]
    states = [state_ref[bb, h] for bb, h, _ in chains]
    zeros = jnp.zeros((C, DN_DV), BF16)
    for c in range(rows // C):
        rs = slice(c * C, (c + 1) * C)
        r1s = [_dot(jnp.concatenate([w_ref[bb, rs, hs], qd_ref[bb, rs, hs]], axis=0), st.astype(BF16))
               for (bb, _, hs), st in zip(chains, states)]
        vbs = [(u_ref[bb, rs, hs] - r1[:C]).astype(BF16) for (bb, _, hs), r1 in zip(chains, r1s)]
        ps = slice((c // 2) * C, (c // 2 + 1) * C)
        vps = [jnp.concatenate([vb, zeros] if c % 2 == 0 else [zeros, vb], axis=0) for vb in vbs]
        outs = [r1[C:] + _dot(intra_ref[h, bb, ps, :], vp) for (bb, h, _), r1, vp in zip(chains, r1s, vps)]
        upds = [_dot_tn(kt_ref[bb, rs, hs], vb) for (bb, _, hs), vb in zip(chains, vbs)]
        for (bb, _, hs), out in zip(chains, outs):
            o_ref[bb, rs, hs] = out.astype(o_ref.dtype)
        states = [st * egl_ref[h, bb, c:c + 1, :] + upd for (bb, h, _), st, upd in zip(chains, states, upds)]
    for (bb, h, _), st in zip(chains, states):
        state_ref[bb, h] = st


def _dn_scan(w, qd, kt, u, intra, egl, batch, seq):
    rows = SCAN_ROWS
    nb = SCAN_BATCH
    as_bs = lambda a: a.reshape(batch, seq, a.shape[-1])
    wide = pl.BlockSpec((nb, rows, DN_QK_W), lambda b, i: (b, i, 0))
    intra4 = intra.reshape(DN_HEADS, batch, seq // 2, 2 * C)
    egl4 = egl.reshape(DN_HEADS, batch, seq // C, LANES)
    out = pl.pallas_call(
        _dn_scan_kernel,
        grid=(batch // nb, seq // rows),
        in_specs=[wide, wide, wide, wide,
                  pl.BlockSpec((DN_HEADS, nb, rows // 2, 2 * C), lambda b, i: (0, b, i, 0)),
                  pl.BlockSpec((DN_HEADS, nb, rows // C, LANES), lambda b, i: (0, b, i, 0))],
        out_specs=wide,
        out_shape=jax.ShapeDtypeStruct((batch, seq, DN_V_W), BF16),
        scratch_shapes=[pltpu.VMEM((nb, DN_HEADS, DN_DK, DN_DV), F32)],
        compiler_params=_params(2),
        name="dn_scan",
    )(as_bs(w), as_bs(qd), as_bs(kt), as_bs(u), intra4, egl4)
    return out.reshape(batch * seq, DN_V_W)


ATT_TQ = 512
ATT_TK = 512
NEG_INF = -1e30


def _mla_attn_kernel(qn_ref, qr_ref, kn_ref, kr_ref, v_ref, o_ref, qcat_ref, kcat_ref, vcat_ref):
    seq = qn_ref.shape[0]
    tq, tk = ATT_TQ, ATT_TK
    qcat_ref[:, :LANES] = qn_ref[...]
    qcat_ref[:, LANES:] = qr_ref[...]
    kcat_ref[:, :LANES] = kn_ref[...]
    kcat_ref[:, LANES:] = kr_ref[...]
    vcat_ref[:, :LANES] = v_ref[...]
    vcat_ref[:, LANES:] = jnp.ones((seq, LANES), BF16)
    causal = (lax.broadcasted_iota(jnp.int32, (tq, tk), 1) <= lax.broadcasted_iota(jnp.int32, (tq, tk), 0))

    for i in range(seq // tq):
        q = qcat_ref[i * tq:(i + 1) * tq, :]
        m = acc = None
        for j in range(i + 1):
            ks = slice(j * tk, (j + 1) * tk)
            s = _dot_nt(q, kcat_ref[ks, :])
            if j == i:
                s = jnp.where(causal, s, NEG_INF)
            m_cur = jnp.max(s, axis=1, keepdims=True)
            if j == 0:
                m_new = jnp.broadcast_to(m_cur, (tq, LANES))
            else:
                m_new = jnp.maximum(m, m_cur)
            p = jnp.exp2(s - jnp.concatenate([m_new] * (tk // LANES), axis=1))
            pv = _dot(p.astype(BF16), vcat_ref[ks, :])
            if j == 0:
                acc = pv
            else:
                alpha = jnp.exp2(m - m_new)
                acc = jnp.concatenate([alpha, alpha], axis=1) * acc + pv
            m = m_new
        o_ref[i * tq:(i + 1) * tq, :] = (acc[:, :LANES] / acc[:, LANES:]).astype(o_ref.dtype)


def _mla_attn(qn, qr, kn, kr, v, batch, seq):
    t = qn.shape[0]
    blk = pl.BlockSpec((seq, LANES), lambda b, h: (b, h))
    return pl.pallas_call(
        _mla_attn_kernel,
        grid=(batch, MLA_HEADS),
        in_specs=[blk, blk, blk, pl.BlockSpec((seq, LANES), lambda b, h: (b, 0)), blk],
        out_specs=blk,
        out_shape=jax.ShapeDtypeStruct((t, MLA_HEADS * MLA_V), BF16),
        scratch_shapes=[pltpu.VMEM((seq, 2 * LANES), BF16)] * 3,
        compiler_params=_params(2),
        name="mla_attn",
    )(qn, qr, kn, kr, v)


MERGE_TM = 512


def _merge_kernel(x_ref, odn_ref, omla_ref, wzg_ref, wdn_ref, wmla_ref, wo_ref, nw_ref, g_ref, b_ref, h_ref):
    x = x_ref[...]
    zg = _dot(x.astype(BF16), wzg_ref[...])
    z = zg[:, :DN_V_W]
    parts = []
    for h in range(DN_HEADS):
        hs = slice(h * DN_DV, (h + 1) * DN_DV)
        parts.append(_rms(odn_ref[:, hs].astype(F32), nw_ref[...]))
    o_dn = jnp.concatenate(parts, axis=1) * _silu(z)
    y_dn = _dot(o_dn.astype(BF16), wdn_ref[...])
    y_mla = _dot(omla_ref[...], wmla_ref[...])
    mixed = _sigmoid(zg[:, DN_V_W:DN_V_W + D_MODEL]) * y_dn + _sigmoid(zg[:, DN_V_W + D_MODEL:]) * y_mla
    r = DEEPNORM_ALPHA * x + _dot(mixed.astype(BF16), wo_ref[...])
    h_ref[...] = _layer_norm(r, g_ref[...], b_ref[...])


def _merge(x2, o_dn, o_mla, wzg, wdn, wmla, wo, nw, g, b):
    t = x2.shape[0]
    tm = MERGE_TM
    row = lambda i: (i, 0)
    act = pl.BlockSpec((tm, D_MODEL), row)
    return pl.pallas_call(
        _merge_kernel,
        grid=(t // tm,),
        in_specs=[act, act, act, _const_spec(wzg.shape), _const_spec(wdn.shape), _const_spec(wmla.shape),
                  _const_spec(wo.shape), _const_spec(nw.shape), _const_spec(g.shape), _const_spec(b.shape)],
        out_specs=act,
        out_shape=jax.ShapeDtypeStruct((t, D_MODEL), F32),
        compiler_params=_params(1),
        name="merge",
    )(x2, o_dn, o_mla, wzg, wdn, wmla, wo, nw, g, b)


FFN_TM = 512
FFN_SLAB = 256


def _ffn_kernel(h_ref, p_ref, win_ref, wout_ref, wpg_ref, wple_ref, g_ref, b_ref, o_ref):
    h = h_ref[...]
    hb = h.astype(BF16)
    acc = jnp.zeros(h.shape, F32)
    for j in range(FFN_HIDDEN // FFN_SLAB):
        cs = slice(j * FFN_SLAB, (j + 1) * FFN_SLAB)
        gt = _dot(hb, win_ref[:, cs])
        up = _dot(hb, win_ref[:, FFN_HIDDEN + j * FFN_SLAB:FFN_HIDDEN + (j + 1) * FFN_SLAB])
        a = (_silu(gt) * up).astype(BF16)
        acc = acc + _dot(a, wout_ref[cs, :])
    ple = _sigmoid(_dot(hb, wpg_ref[...])) * _dot(p_ref[...].astype(BF16), wple_ref[...])
    r = DEEPNORM_ALPHA * h + acc + ple
    o_ref[...] = _layer_norm(r, g_ref[...], b_ref[...])


def _ffn(h1, p2, win, wout, wpg, wple, g, b):
    assert FFN_HIDDEN % FFN_SLAB == 0
    t = h1.shape[0]
    tm = FFN_TM
    row = lambda i: (i, 0)
    act = pl.BlockSpec((tm, D_MODEL), row)
    return pl.pallas_call(
        _ffn_kernel,
        grid=(t // tm,),
        in_specs=[act, pl.BlockSpec((tm, PLE_DIM), row), _const_spec(win.shape), _const_spec(wout.shape),
                  _const_spec(wpg.shape), _const_spec(wple.shape), _const_spec(g.shape), _const_spec(b.shape)],
        out_specs=act,
        out_shape=jax.ShapeDtypeStruct((t, D_MODEL), F32),
        compiler_params=_params(1),
        name="ffn",
    )(h1, p2, win, wout, wpg, wple, g, b)


def _split_w_in(w_in):
    offs = [0]
    for n in IN_SIZES:
        offs.append(offs[-1] + n)
    return [w_in[:, offs[i]:offs[i + 1]] for i in range(len(IN_SIZES))]


def kernel(x, p, positions, w_in, conv_w, dn_a_log, dn_dt_bias, dn_norm_w, q_norm_w, w_uq, kv_norm_w, w_uk, w_uv,
           w_br_dn, w_br_mla, w_o, ln1_g, ln1_b, w_ffn_in, w_ffn_out, w_ple, w_ple_gate, ln2_g, ln2_b):
    assert w_in.shape[0] == DEPTH
    batch, seq, _ = x.shape
    t = batch * seq
    h = x.reshape(t, D_MODEL)
    pos2 = positions.reshape(t, 1)
    half = MLA_ROPE // 2
    invf = ROPE_BASE ** (-jnp.arange(0, MLA_ROPE, 2, dtype=F32) / MLA_ROPE)
    invf = jnp.tile(invf, LANES // half).reshape(1, LANES)

    for i in range(DEPTH):
        w_in_bf = w_in[i].astype(BF16)
        _, w_z, w_b, w_a, w_cq, w_ckv, w_kr, w_gdn, w_gmla = _split_w_in(w_in_bf)
        w_kr_sw = jnp.concatenate([w_kr[:, half:], w_kr[:, :half]], axis=1)
        pad = jnp.zeros((D_MODEL, LAT_W - LAT_BA - 2 * DN_HEADS), BF16)
        w_lat = jnp.concatenate([w_cq, w_ckv, w_kr, w_kr_sw, w_b, w_a, pad], axis=1)
        w_zg = jnp.concatenate([w_z, w_gdn, w_gmla], axis=1)
        uq = w_uq[i]
        t1 = uq[:, :, MLA_NOPE:MLA_NOPE + half]
        t2 = uq[:, :, MLA_NOPE + half:]
        wqn = uq[:, :, :MLA_NOPE].reshape(MLA_Q_LORA, MLA_HEADS * MLA_NOPE).astype(BF16)
        wqr = jnp.concatenate([t1, t2, t2, t1], axis=2).reshape(MLA_Q_LORA, MLA_HEADS * LANES).astype(BF16)
        wuk = w_uk[i].reshape(MLA_KV_LORA, MLA_HEADS * MLA_NOPE).astype(BF16)
        wuv = w_uv[i].reshape(MLA_KV_LORA, MLA_HEADS * MLA_V).astype(BF16)
        lane_pad = jnp.zeros((LANES - 2 * DN_HEADS,), F32)
        head_pad = jnp.zeros((DN_HEADS,), F32)
        dnp = jnp.stack([jnp.concatenate([head_pad, dn_a_log[i], lane_pad]),
                         jnp.concatenate([head_pad, dn_dt_bias[i], lane_pad])])
        nw = jnp.tile(dn_norm_w[i], 1).reshape(1, DN_DV)
        w_ffn = w_ffn_in[i].astype(BF16)

        q, k, v, qn, qr, kn, kr, vm, bg = _input_stage(
            h, w_in_bf, conv_w[i], w_lat, pos2, invf, q_norm_w[i].reshape(1, -1), kv_norm_w[i].reshape(1, -1),
            wqn, wqr, wuk, wuv, dnp, seq)
        w, qd, kt, u, intra, egl = _dn_prep(q, k, v, bg)
        o_dn = _dn_scan(w, qd, kt, u, intra, egl, batch, seq)
        o_mla = _mla_attn(qn, qr, kn, kr, vm, batch, seq)
        h1 = _merge(h, o_dn, o_mla, w_zg, w_br_dn[i].astype(BF16), w_br_mla[i].astype(BF16),
                    w_o[i].astype(BF16), nw, ln1_g[i].reshape(1, -1), ln1_b[i].reshape(1, -1))
        h = _ffn(h1, p[i].reshape(t, PLE_DIM), w_ffn,
                 w_ffn_out[i].astype(BF16), w_ple_gate[i].astype(BF16), w_ple[i].astype(BF16),
                 ln2_g[i].reshape(1, -1), ln2_b[i].reshape(1, -1))
    return h.reshape(batch, seq, D_MODEL)
```

```python
import functools
import math

import jax
import jax.numpy as jnp
from jax import lax
from jax.experimental import pallas as pl
from jax.experimental.pallas import tpu as pltpu

D_MODEL = 1024
PLE_DIM = 256
DN_HEADS = 8
DN_DK = 128
DN_DV = 128
DN_CONV = 4
DN_CHUNK = 64
DN_QK_W = DN_HEADS * DN_DK
DN_V_W = DN_HEADS * DN_DV
DN_CONV_CH = 2 * DN_QK_W + DN_V_W
MLA_HEADS = 8
MLA_Q_LORA = 384
MLA_KV_LORA = 256
MLA_NOPE = 128
MLA_ROPE = 64
MLA_V = 128
ROPE_BASE = 10000.0
FFN_HIDDEN = -(-8 * D_MODEL // (3 * 256)) * 256
DEPTH = 1
DEEPNORM_ALPHA = (2.0 * DEPTH) ** 0.25
IN_SIZES = (DN_CONV_CH, DN_V_W, DN_HEADS, DN_HEADS, MLA_Q_LORA, MLA_KV_LORA, MLA_ROPE, D_MODEL, D_MODEL)

LANES = 128
SUBLANES = 8
VMEM_LIMIT_BYTES = 56 * 1024 * 1024

LAT_W = 896
LAT_CKV = MLA_Q_LORA
LAT_KR = LAT_CKV + MLA_KV_LORA
LAT_BA = LAT_KR + 2 * MLA_ROPE

BF16 = jnp.bfloat16
F32 = jnp.float32


def _dot(a, b):
    return jnp.dot(a, b, preferred_element_type=F32)


def _dot_nt(a, b):
    return lax.dot_general(a, b, (((1,), (1,)), ((), ())), preferred_element_type=F32)


def _dot_tn(a, b):
    return lax.dot_general(a, b, (((0,), (0,)), ((), ())), preferred_element_type=F32)


def _sigmoid(x):
    return 0.5 * jnp.tanh(0.5 * x) + 0.5


def _silu(x):
    h = 0.5 * x
    return h * jnp.tanh(h) + h


def _layer_norm(r, g, b):
    mu = jnp.mean(r, axis=-1, keepdims=True)
    c = r - mu
    var = jnp.mean(c * c, axis=-1, keepdims=True)
    return c * lax.rsqrt(var + 1e-5) * g + b


def _const_spec(shape):
    return pl.BlockSpec(shape, lambda *_: (0,) * len(shape))


def _params(n_axes, sem=None):
    return pltpu.CompilerParams(
        dimension_semantics=sem or ("arbitrary",) * n_axes, vmem_limit_bytes=VMEM_LIMIT_BYTES)


QKV_TM = 512
QKV_SLAB = 512
QKV_ROWS = 64


def _dn_qkv_kernel(tiles_per_seq, x_ref, w_ref, cw_ref, q_ref, k_ref, v_ref, pre_ref, tail_ref):
    i = pl.program_id(0)
    tm = x_ref.shape[0]
    assert DN_CONV == 4

    @pl.when(i % tiles_per_seq == 0)
    def _():
        pre_ref[:SUBLANES, :] = jnp.zeros((SUBLANES, DN_CONV_CH), F32)
        tail_ref[...] = jnp.zeros_like(tail_ref)

    xb = x_ref[...].astype(BF16)
    outs = (q_ref, k_ref, v_ref)
    slabs_per_out = DN_QK_W // QKV_SLAB
    for j in range(DN_CONV_CH // QKV_SLAB):
        cs = slice(j * QKV_SLAB, (j + 1) * QKV_SLAB)
        pre_ref[SUBLANES:, cs] = _dot(xb, w_ref[:, cs])
        cw = 0.5 * cw_ref[:, cs]
        out_ref = outs[j // slabs_per_out]
        col0 = (j % slabs_per_out) * QKV_SLAB
        which = j // slabs_per_out
        pair_tail = tail_ref[:, cs]
        for r in range(tm // QKV_ROWS):
            ext = pre_ref[r * QKV_ROWS:(r + 1) * QKV_ROWS + SUBLANES, cs]
            pre = ext[SUBLANES:]
            prev = pltpu.roll(ext, 1, axis=0)[SUBLANES:]
            pair = pre * cw[1:2] + prev * cw[0:1]
            pair2 = pltpu.roll(jnp.concatenate([pair_tail, pair], axis=0), 2, axis=0)[SUBLANES:]
            pair_tail = pair[QKV_ROWS - SUBLANES:]
            half_y = pre * cw[3:4] + prev * cw[2:3] + pair2
            y = half_y * jnp.tanh(half_y) + half_y
            rs = slice(r * QKV_ROWS, (r + 1) * QKV_ROWS)
            for hh in range(QKV_SLAB // DN_DK):
                yh = y[:, hh * DN_DK:(hh + 1) * DN_DK]
                if which < 2:
                    ss = jnp.sum(yh * yh, axis=-1, keepdims=True)
                    scale = lax.rsqrt(ss + 1e-6)
                    yh = yh * (scale * (DN_DK ** -0.5) if which == 0 else scale)
                out_ref[rs, col0 + hh * DN_DK:col0 + (hh + 1) * DN_DK] = yh.astype(out_ref.dtype)
        tail_ref[:, cs] = pair_tail
        pre_ref[:SUBLANES, cs] = pre_ref[tm:, cs]


MLA_SCALE = (MLA_NOPE + MLA_ROPE) ** -0.5 * math.log2(math.e)


def _rms(t, w):
    return t * lax.rsqrt(jnp.mean(t * t, axis=-1, keepdims=True) + 1e-6) * w


def _latent_kernel(x_ref, wl_ref, pos_ref, invf_ref, qnw_ref, kvw_ref, wqn_ref, wqr_ref, wuk_ref, wuv_ref,
                   dnp_ref, qn_ref, qr_ref, kn_ref, kr_ref, v_ref, bg_ref):
    xb = x_ref[...].astype(BF16)
    lat = _dot(xb, wl_ref[...])

    tm = x_ref.shape[0]
    qt = tm // 4
    half = MLA_ROPE // 2
    lane = lax.broadcasted_iota(jnp.int32, (1, LANES), 1)
    pos = pos_ref[...].astype(F32)
    grp = lane // half
    pos4 = jnp.where(grp == 0, pos[:qt], jnp.where(grp == 1, pos[qt:2 * qt],
                                                   jnp.where(grp == 2, pos[2 * qt:3 * qt], pos[3 * qt:])))
    ang = pos4 * invf_ref[...]
    cos3 = jnp.concatenate(_split3(jnp.cos(ang)), axis=1)
    sin3 = jnp.concatenate(_split3(jnp.sin(ang)), axis=1)
    src = lax.broadcasted_iota(jnp.int32, (3 * LANES, LANES), 0) % LANES
    dst = lax.broadcasted_iota(jnp.int32, (3 * LANES, LANES), 1)
    cos_parts, sin_parts = [], []
    for g in range(4):
        hit = (src // half == g) & (src % half == dst % half) & (dst < MLA_ROPE)
        cos_parts.append(_dot(cos3, jnp.where(hit, 1.0, 0.0).astype(BF16)))
        sin_parts.append(_dot(sin3, jnp.where(hit, jnp.where(dst < half, -1.0, 1.0), 0.0).astype(BF16)))
    cos_t = jnp.concatenate(cos_parts, axis=0)
    sin_t = jnp.concatenate(sin_parts, axis=0)

    def rope(a):
        return a * cos_t + pltpu.roll(a, MLA_ROPE, axis=1) * sin_t

    c_q = _rms(lat[:, :MLA_Q_LORA], qnw_ref[...] * MLA_SCALE).astype(BF16)
    qn_ref[...] = _dot(c_q, wqn_ref[...]).astype(qn_ref.dtype)
    qr_raw = _dot(c_q, wqr_ref[...])
    for h in range(MLA_HEADS):
        hs = slice(h * LANES, (h + 1) * LANES)
        qr_ref[:, hs] = rope(qr_raw[:, hs]).astype(qr_ref.dtype)

    c_kv = _rms(lat[:, LAT_CKV:LAT_KR], kvw_ref[...]).astype(BF16)
    kn_ref[...] = _dot(c_kv, wuk_ref[...]).astype(kn_ref.dtype)
    v_ref[...] = _dot(c_kv, wuv_ref[...]).astype(v_ref.dtype)
    kr_ref[...] = rope(lat[:, LAT_KR:LAT_BA]).astype(kr_ref.dtype)

    ba = lat[:, LAT_BA:LAT_BA + LANES]
    beta = _sigmoid(ba)
    sp_in = ba + dnp_ref[1:2, :]
    softplus = jnp.maximum(sp_in, 0.0) + jnp.log(1.0 + jnp.exp(-jnp.abs(sp_in)))
    g = -jnp.exp(dnp_ref[0:1, :]) * softplus
    bg_ref[...] = jnp.where(lane < DN_HEADS, beta, jnp.where(lane < 2 * DN_HEADS, g, 0.0))


def _input_kernel(tiles_per_seq, x_ref, wqkv_ref, cw_ref, wl_ref, pos_ref, invf_ref, qnw_ref, kvw_ref, wqn_ref,
                  wqr_ref, wuk_ref, wuv_ref, dnp_ref, q_ref, k_ref, v_ref, qn_ref, qr_ref, kn_ref, kr_ref, vm_ref,
                  bg_ref, pre_ref, tail_ref):
    _dn_qkv_kernel(tiles_per_seq, x_ref, wqkv_ref, cw_ref, q_ref, k_ref, v_ref, pre_ref, tail_ref)
    _latent_kernel(x_ref, wl_ref, pos_ref, invf_ref, qnw_ref, kvw_ref, wqn_ref, wqr_ref, wuk_ref, wuv_ref, dnp_ref,
                   qn_ref, qr_ref, kn_ref, kr_ref, vm_ref, bg_ref)


def _input_stage(x2, w_in_bf, conv_w, w_lat, pos2, invf, qnw, kvw, wqn, wqr, wuk, wuv, dnp, seq):
    t = x2.shape[0]
    tm = QKV_TM
    row = lambda i: (i, 0)
    dn_out = jax.ShapeDtypeStruct((t, DN_QK_W), BF16)
    wide = jax.ShapeDtypeStruct((t, MLA_HEADS * LANES), BF16)
    wide_spec = pl.BlockSpec((tm, MLA_HEADS * LANES), row)
    lane_spec = pl.BlockSpec((tm, LANES), row)
    consts = [conv_w, w_lat]
    consts2 = [invf, qnw, kvw, wqn, wqr, wuk, wuv, dnp]
    return pl.pallas_call(
        functools.partial(_input_kernel, seq // tm),
        grid=(t // tm,),
        in_specs=[pl.BlockSpec((tm, D_MODEL), row), _const_spec((D_MODEL, DN_CONV_CH))]
        + [_const_spec(a.shape) for a in consts] + [pl.BlockSpec((tm, 1), row)]
        + [_const_spec(a.shape) for a in consts2],
        out_specs=[pl.BlockSpec((tm, DN_QK_W), row)] * 3 + [wide_spec] * 3 + [lane_spec, wide_spec, lane_spec],
        out_shape=[dn_out] * 3 + [wide, wide, wide, jax.ShapeDtypeStruct((t, LANES), BF16), wide,
                                  jax.ShapeDtypeStruct((t, LANES), F32)],
        scratch_shapes=[pltpu.VMEM((SUBLANES + tm, DN_CONV_CH), F32), pltpu.VMEM((SUBLANES, DN_CONV_CH), F32)],
        compiler_params=_params(1),
        name="input_stage",
    )(x2, w_in_bf, conv_w, w_lat, pos2, invf, qnw, kvw, wqn, wqr, wuk, wuv, dnp)


PREP_ROWS = 512
PREP_HEADS = 8
C = DN_CHUNK


def _split3(a):
    a1 = a.astype(BF16)
    r1 = a - a1.astype(F32)
    a2 = r1.astype(BF16)
    a3 = (r1 - a2.astype(F32)).astype(BF16)
    return a1, a2, a3


def _pair_blockdiag(m):
    lane = lax.broadcasted_iota(jnp.int32, m.shape, 1)
    zero = jnp.zeros_like(m)
    return jnp.concatenate([jnp.where(lane < C, m, zero), jnp.where(lane >= C, m, zero)], axis=0)


def _pair_mm(a, b):
    return _dot(a.astype(BF16), _pair_blockdiag(b.astype(BF16)))


def _dn_prep_kernel(q_ref, k_ref, v_ref, bg_ref, w_ref, qd_ref, kt_ref, u_ref, intra_ref, egl_ref):
    assert PREP_HEADS == DN_HEADS
    rows = q_ref.shape[0]
    nc = rows // C
    pc = 2 * C

    row = lax.broadcasted_iota(jnp.int32, (C, pc), 0)
    lane = lax.broadcasted_iota(jnp.int32, (C, pc), 1)
    col = lane % C
    is_a = lane < C
    tril = row >= col
    strict = row > col
    eye = (row == col).astype(F32)
    tril3 = (lax.broadcasted_iota(jnp.int32, (C, 3 * C), 1) % C
             <= lax.broadcasted_iota(jnp.int32, (C, 3 * C), 0)).astype(BF16)

    bg = bg_ref[...]
    betas = [jnp.broadcast_to(bg[:, hh:hh + 1], (rows, LANES)) for hh in range(DN_HEADS)]

    chunks = [slice(c * C, (c + 1) * C) for c in range(nc)]
    units = [(hh, p) for hh in range(PREP_HEADS) for p in range(nc // 2)]
    cums = [_dot(tril3, jnp.concatenate(_split3(bg[rs, :]), axis=0)) for rs in chunks]
    gcs = [[jnp.broadcast_to(cum[:, DN_HEADS + hh:DN_HEADS + hh + 1], (C, LANES)) for cum in cums]
           for hh in range(DN_HEADS)]
    pick_t = (lax.broadcasted_iota(jnp.int32, (2 * SUBLANES, 3 * LANES), 1) % LANES
              == lax.broadcasted_iota(jnp.int32, (2 * SUBLANES, 3 * LANES), 0) + DN_HEADS).astype(BF16)
    cum_t = [_dot_nt(pick_t, jnp.concatenate([jnp.concatenate(_split3(cums[2 * p]), axis=1),
                                              jnp.concatenate(_split3(cums[2 * p + 1]), axis=1)], axis=0))
             for p in range(nc // 2)]
    rowforms = [jnp.broadcast_to(cum_t[p][hh:hh + 1, :], (C, pc)) for hh, p in units]
    decays, kbs, l_mats, qks = [], [], [], []
    for (hh, p), rowform in zip(units, rowforms):
        rs, hs = slice(p * pc, (p + 1) * pc), slice(hh * DN_DK, (hh + 1) * DN_DK)
        diff = jnp.where(is_a, gcs[hh][2 * p], gcs[hh][2 * p + 1]) - rowform
        decay = jnp.where(tril, jnp.exp(jnp.where(tril, diff, 0.0)), 0.0)
        k = k_ref[rs, hs]
        kb = k * betas[hh][rs, :]
        kbb, qb = kb.astype(BF16), q_ref[rs, hs].astype(BF16)
        res = _dot_nt(jnp.concatenate([kbb[:C], qb[:C], kbb[C:], qb[C:]], axis=0), k.astype(BF16))
        decays.append(decay)
        kbs.append(kb)
        l_mats.append(jnp.where(strict, jnp.where(is_a, res[:C], res[2 * C:3 * C]) * decay, 0.0))
        qks.append(jnp.where(is_a, res[C:2 * C], res[3 * C:]))

    invs = [eye] * len(units)
    s = 1
    while s < C:
        blk = ((row // (2 * s)) == (col // (2 * s))) & ((row % (2 * s)) >= s) & ((col % (2 * s)) < s)
        offs = [jnp.where(blk, l, 0.0) for l in l_mats]
        if s == 1:
            invs = [inv - off for inv, off in zip(invs, offs)]
        else:
            mids = [_pair_mm(off, inv) for off, inv in zip(offs, invs)]
            invs = [inv - _pair_mm(inv, mid) for inv, mid in zip(invs, mids)]
        s *= 2

    for n, (hh, p) in enumerate(units):
        rs, hs = slice(p * pc, (p + 1) * pc), slice(hh * DN_DK, (hh + 1) * DN_DK)
        ga, gb = gcs[hh][2 * p], gcs[hh][2 * p + 1]
        egc = jnp.exp(jnp.concatenate([ga, gb], axis=0))
        rhs = jnp.concatenate([(v_ref[rs, hs] * betas[hh][rs, :]).astype(BF16), (kbs[n] * egc).astype(BF16)], axis=1)
        uw = _dot(_pair_blockdiag(invs[n].astype(BF16)), rhs)
        u_ref[rs, hs] = uw[:, :DN_DV].astype(u_ref.dtype)
        w_ref[rs, hs] = uw[:, DN_DV:].astype(w_ref.dtype)
        intra_ref[hh, p * C:(p + 1) * C, :] = jnp.where(tril, qks[n] * decays[n], 0.0).astype(intra_ref.dtype)
        qd_ref[rs, hs] = (q_ref[rs, hs] * egc).astype(qd_ref.dtype)
        gla, glb = ga[C - 1:C, :], gb[C - 1:C, :]
        tail = jnp.exp(jnp.concatenate([gla - ga, glb - gb], axis=0))
        kt_ref[rs, hs] = (k_ref[rs, hs] * tail).astype(kt_ref.dtype)
        egl_ref[hh, 2 * p:2 * p + 1, :] = jnp.exp(gla)
        egl_ref[hh, 2 * p + 1:2 * p + 2, :] = jnp.exp(glb)


def _dn_prep(q, k, v, bg):
    t = q.shape[0]
    rows = PREP_ROWS
    blk = pl.BlockSpec((rows, PREP_HEADS * DN_DK), lambda i, h: (i, h))
    wide_bf = jax.ShapeDtypeStruct((t, DN_QK_W), BF16)
    return pl.pallas_call(
        _dn_prep_kernel,
        grid=(t // rows, DN_HEADS // PREP_HEADS),
        in_specs=[blk, blk, blk, pl.BlockSpec((rows, LANES), lambda i, h: (i, 0))],
        out_specs=[blk, blk, blk, blk, pl.BlockSpec((PREP_HEADS, rows // 2, 2 * C), lambda i, h: (h, i, 0)),
                   pl.BlockSpec((PREP_HEADS, rows // C, LANES), lambda i, h: (h, i, 0))],
        out_shape=[wide_bf, wide_bf, wide_bf, wide_bf,
                   jax.ShapeDtypeStruct((DN_HEADS, t // 2, 2 * C), BF16),
                   jax.ShapeDtypeStruct((DN_HEADS, t // C, LANES), F32)],
        compiler_params=_params(2),
        name="dn_prep",
    )(q, k, v, bg)


SCAN_ROWS = 512
SCAN_BATCH = 2


def _dn_scan_kernel(w_ref, qd_ref, kt_ref, u_ref, intra_ref, egl_ref, o_ref, state_ref):
    @pl.when(pl.program_id(1) == 0)
    def _():
        state_ref[...] = jnp.zeros_like(state_ref)

    nb, rows = w_ref.shape[0], w_ref.shape[1]
    chains = [(bb, h, slice(h * DN_DK, (h + 1) * DN_DK)) for bb in range(nb) for h in range(DN_HEADS)]
    states = [state_ref[bb, h] for bb, h, _ in chains]
    zeros = jnp.zeros((C, DN_DV), BF16)
    for c in range(rows // C):
        rs = slice(c * C, (c + 1) * C)
        r1s = [_dot(jnp.concatenate([w_ref[bb, rs, hs], qd_ref[bb, rs, hs]], axis=0), st.astype(BF16))
               for (bb, _, hs), st in zip(chains, states)]
        vbs = [(u_ref[bb, rs, hs] - r1[:C]).astype(BF16) for (bb, _, hs), r1 in zip(chains, r1s)]
        ps = slice((c // 2) * C, (c // 2 + 1) * C)
        vps = [jnp.concatenate([vb, zeros] if c % 2 == 0 else [zeros, vb], axis=0) for vb in vbs]
        outs = [r1[C:] + _dot(intra_ref[h, bb, ps, :], vp) for (bb, h, _), r1, vp in zip(chains, r1s, vps)]
        upds = [_dot_tn(kt_ref[bb, rs, hs], vb) for (bb, _, hs), vb in zip(chains, vbs)]
        for (bb, _, hs), out in zip(chains, outs):
            o_ref[bb, rs, hs] = out.astype(o_ref.dtype)
        states = [st * egl_ref[h, bb, c:c + 1, :] + upd for (bb, h, _), st, upd in zip(chains, states, upds)]
    for (bb, h, _), st in zip(chains, states):
        state_ref[bb, h] = st


def _dn_scan(w, qd, kt, u, intra, egl, batch, seq):
    rows = SCAN_ROWS
    nb = SCAN_BATCH
    as_bs = lambda a: a.reshape(batch, seq, a.shape[-1])
    wide = pl.BlockSpec((nb, rows, DN_QK_W), lambda b, i: (b, i, 0))
    intra4 = intra.reshape(DN_HEADS, batch, seq // 2, 2 * C)
    egl4 = egl.reshape(DN_HEADS, batch, seq // C, LANES)
    out = pl.pallas_call(
        _dn_scan_kernel,
        grid=(batch // nb, seq // rows),
        in_specs=[wide, wide, wide, wide,
                  pl.BlockSpec((DN_HEADS, nb, rows // 2, 2 * C), lambda b, i: (0, b, i, 0)),
                  pl.BlockSpec((DN_HEADS, nb, rows // C, LANES), lambda b, i: (0, b, i, 0))],
        out_specs=wide,
        out_shape=jax.ShapeDtypeStruct((batch, seq, DN_V_W), BF16),
        scratch_shapes=[pltpu.VMEM((nb, DN_HEADS, DN_DK, DN_DV), F32)],
        compiler_params=_params(2),
        name="dn_scan",
    )(as_bs(w), as_bs(qd), as_bs(kt), as_bs(u), intra4, egl4)
    return out.reshape(batch * seq, DN_V_W)


ATT_TQ = 512
ATT_TK = 512
NEG_INF = -1e30


def _mla_attn_kernel(qn_ref, qr_ref, kn_ref, kr_ref, v_ref, o_ref, qcat_ref, kcat_ref, vcat_ref):
    seq = qn_ref.shape[0]
    tq, tk = ATT_TQ, ATT_TK
    qcat_ref[:, :LANES] = qn_ref[...]
    qcat_ref[:, LANES:] = qr_ref[...]
    kcat_ref[:, :LANES] = kn_ref[...]
    kcat_ref[:, LANES:] = kr_ref[...]
    vcat_ref[:, :LANES] = v_ref[...]
    vcat_ref[:, LANES:] = jnp.ones((seq, LANES), BF16)
    causal = (lax.broadcasted_iota(jnp.int32, (tq, tk), 1) <= lax.broadcasted_iota(jnp.int32, (tq, tk), 0))

    for i in range(seq // tq):
        q = qcat_ref[i * tq:(i + 1) * tq, :]
        m = acc = None
        for j in range(i + 1):
            ks = slice(j * tk, (j + 1) * tk)
            s = _dot_nt(q, kcat_ref[ks, :])
            if j == i:
                s = jnp.where(causal, s, NEG_INF)
            m_cur = jnp.max(s, axis=1, keepdims=True)
            if j == 0:
                m_new = jnp.broadcast_to(m_cur, (tq, LANES))
            else:
                m_new = jnp.maximum(m, m_cur)
            p = jnp.exp2(s - jnp.concatenate([m_new] * (tk // LANES), axis=1))
            pv = _dot(p.astype(BF16), vcat_ref[ks, :])
            if j == 0:
                acc = pv
            else:
                alpha = jnp.exp2(m - m_new)
                acc = jnp.concatenate([alpha, alpha], axis=1) * acc + pv
            m = m_new
        o_ref[i * tq:(i + 1) * tq, :] = (acc[:, :LANES] / acc[:, LANES:]).astype(o_ref.dtype)


def _mla_attn(qn, qr, kn, kr, v, batch, seq):
    t = qn.shape[0]
    blk = pl.BlockSpec((seq, LANES), lambda b, h: (b, h))
    return pl.pallas_call(
        _mla_attn_kernel,
        grid=(batch, MLA_HEADS),
        in_specs=[blk, blk, blk, pl.BlockSpec((seq, LANES), lambda b, h: (b, 0)), blk],
        out_specs=blk,
        out_shape=jax.ShapeDtypeStruct((t, MLA_HEADS * MLA_V), BF16),
        scratch_shapes=[pltpu.VMEM((seq, 2 * LANES), BF16)] * 3,
        compiler_params=_params(2),
        name="mla_attn",
    )(qn, qr, kn, kr, v)


MERGE_TM = 512


def _merge_kernel(x_ref, odn_ref, omla_ref, wzg_ref, wdn_ref, wmla_ref, wo_ref, nw_ref, g_ref, b_ref, h_ref):
    x = x_ref[...]
    zg = _dot(x.astype(BF16), wzg_ref[...])
    z = zg[:, :DN_V_W]
    parts = []
    for h in range(DN_HEADS):
        hs = slice(h * DN_DV, (h + 1) * DN_DV)
        parts.append(_rms(odn_ref[:, hs].astype(F32), nw_ref[...]))
    o_dn = jnp.concatenate(parts, axis=1) * _silu(z)
    y_dn = _dot(o_dn.astype(BF16), wdn_ref[...])
    y_mla = _dot(omla_ref[...], wmla_ref[...])
    mixed = _sigmoid(zg[:, DN_V_W:DN_V_W + D_MODEL]) * y_dn + _sigmoid(zg[:, DN_V_W + D_MODEL:]) * y_mla
    r = DEEPNORM_ALPHA * x + _dot(mixed.astype(BF16), wo_ref[...])
    h_ref[...] = _layer_norm(r, g_ref[...], b_ref[...])


def _merge(x2, o_dn, o_mla, wzg, wdn, wmla, wo, nw, g, b):
    t = x2.shape[0]
    tm = MERGE_TM
    row = lambda i: (i, 0)
    act = pl.BlockSpec((tm, D_MODEL), row)
    return pl.pallas_call(
        _merge_kernel,
        grid=(t // tm,),
        in_specs=[act, act, act, _const_spec(wzg.shape), _const_spec(wdn.shape), _const_spec(wmla.shape),
                  _const_spec(wo.shape), _const_spec(nw.shape), _const_spec(g.shape), _const_spec(b.shape)],
        out_specs=act,
        out_shape=jax.ShapeDtypeStruct((t, D_MODEL), F32),
        compiler_params=_params(1),
        name="merge",
    )(x2, o_dn, o_mla, wzg, wdn, wmla, wo, nw, g, b)


FFN_TM = 512
FFN_SLAB = 256


def _ffn_kernel(h_ref, p_ref, win_ref, wout_ref, wpg_ref, wple_ref, g_ref, b_ref, o_ref):
    h = h_ref[...]
    hb = h.astype(BF16)
    acc = jnp.zeros(h.shape, F32)
    for j in range(FFN_HIDDEN // FFN_SLAB):
        cs = slice(j * FFN_SLAB, (j + 1) * FFN_SLAB)
        gt = _dot(hb, win_ref[:, cs])
        up = _dot(hb, win_ref[:, FFN_HIDDEN + j * FFN_SLAB:FFN_HIDDEN + (j + 1) * FFN_SLAB])
        a = (_silu(gt) * up).astype(BF16)
        acc = acc + _dot(a, wout_ref[cs, :])
    ple = _sigmoid(_dot(hb, wpg_ref[...])) * _dot(p_ref[...].astype(BF16), wple_ref[...])
    r = DEEPNORM_ALPHA * h + acc + ple
    o_ref[...] = _layer_norm(r, g_ref[...], b_ref[...])


def _ffn(h1, p2, win, wout, wpg, wple, g, b):
    assert FFN_HIDDEN % FFN_SLAB == 0
    t = h1.shape[0]
    tm = FFN_TM
    row = lambda i: (i, 0)
    act = pl.BlockSpec((tm, D_MODEL), row)
    return pl.pallas_call(
        _ffn_kernel,
        grid=(t // tm,),
        in_specs=[act, pl.BlockSpec((tm, PLE_DIM), row), _const_spec(win.shape), _const_spec(wout.shape),
                  _const_spec(wpg.shape), _const_spec(wple.shape), _const_spec(g.shape), _const_spec(b.shape)],
        out_specs=act,
        out_shape=jax.ShapeDtypeStruct((t, D_MODEL), F32),
        compiler_params=_params(1),
        name="ffn",
    )(h1, p2, win, wout, wpg, wple, g, b)


def _split_w_in(w_in):
    offs = [0]
    for n in IN_SIZES:
        offs.append(offs[-1] + n)
    return [w_in[:, offs[i]:offs[i + 1]] for i in range(len(IN_SIZES))]


def kernel(x, p, positions, w_in, conv_w, dn_a_log, dn_dt_bias, dn_norm_w, q_norm_w, w_uq, kv_norm_w, w_uk, w_uv,
           w_br_dn, w_br_mla, w_o, ln1_g, ln1_b, w_ffn_in, w_ffn_out, w_ple, w_ple_gate, ln2_g, ln2_b):
    assert w_in.shape[0] == DEPTH
    batch, seq, _ = x.shape
    t = batch * seq
    h = x.reshape(t, D_MODEL)
    pos2 = positions.reshape(t, 1)
    half = MLA_ROPE // 2
    invf = ROPE_BASE ** (-jnp.arange(0, MLA_ROPE, 2, dtype=F32) / MLA_ROPE)
    invf = jnp.tile(invf, LANES // half).reshape(1, LANES)

    for i in range(DEPTH):
        w_in_bf = w_in[i].astype(BF16)
        _, w_z, w_b, w_a, w_cq, w_ckv, w_kr, w_gdn, w_gmla = _split_w_in(w_in_bf)
        w_kr_sw = jnp.concatenate([w_kr[:, half:], w_kr[:, :half]], axis=1)
        pad = jnp.zeros((D_MODEL, LAT_W - LAT_BA - 2 * DN_HEADS), BF16)
        w_lat = jnp.concatenate([w_cq, w_ckv, w_kr, w_kr_sw, w_b, w_a, pad], axis=1)
        w_zg = jnp.concatenate([w_z, w_gdn, w_gmla], axis=1)
        uq = w_uq[i]
        t1 = uq[:, :, MLA_NOPE:MLA_NOPE + half]
        t2 = uq[:, :, MLA_NOPE + half:]
        wqn = uq[:, :, :MLA_NOPE].reshape(MLA_Q_LORA, MLA_HEADS * MLA_NOPE).astype(BF16)
        wqr = jnp.concatenate([t1, t2, t2, t1], axis=2).reshape(MLA_Q_LORA, MLA_HEADS * LANES).astype(BF16)
        wuk = w_uk[i].reshape(MLA_KV_LORA, MLA_HEADS * MLA_NOPE).astype(BF16)
        wuv = w_uv[i].reshape(MLA_KV_LORA, MLA_HEADS * MLA_V).astype(BF16)
        lane_pad = jnp.zeros((LANES - 2 * DN_HEADS,), F32)
        head_pad = jnp.zeros((DN_HEADS,), F32)
        dnp = jnp.stack([jnp.concatenate([head_pad, dn_a_log[i], lane_pad]),
                         jnp.concatenate([head_pad, dn_dt_bias[i], lane_pad])])
        nw = jnp.tile(dn_norm_w[i], 1).reshape(1, DN_DV)
        w_ffn = w_ffn_in[i].astype(BF16)

        q, k, v, qn, qr, kn, kr, vm, bg = _input_stage(
            h, w_in_bf, conv_w[i], w_lat, pos2, invf, q_norm_w[i].reshape(1, -1), kv_norm_w[i].reshape(1, -1),
            wqn, wqr, wuk, wuv, dnp, seq)
        w, qd, kt, u, intra, egl = _dn_prep(q, k, v, bg)
        o_dn = _dn_scan(w, qd, kt, u, intra, egl, batch, seq)
        o_mla = _mla_attn(qn, qr, kn, kr, vm, batch, seq)
        h1 = _merge(h, o_dn, o_mla, w_zg, w_br_dn[i].astype(BF16), w_br_mla[i].astype(BF16),
                    w_o[i].astype(BF16), nw, ln1_g[i].reshape(1, -1), ln1_b[i].reshape(1, -1))
        h = _ffn(h1, p[i].reshape(t, PLE_DIM), w_ffn,
                 w_ffn_out[i].astype(BF16), w_ple_gate[i].astype(BF16), w_ple[i].astype(BF16),
                 ln2_g[i].reshape(1, -1), ln2_b[i].reshape(1, -1))
    return h.reshape(batch, seq, D_MODEL)
```

```python
import functools
import math

import jax
import jax.numpy as jnp
from jax import lax
from jax.experimental import pallas as pl
from jax.experimental.pallas import tpu as pltpu

D_MODEL = 1024
PLE_DIM = 256
DN_HEADS = 8
DN_DK = 128
DN_DV = 128
DN_CONV = 4
DN_CHUNK = 64
DN_QK_W = DN_HEADS * DN_DK
DN_V_W = DN_HEADS * DN_DV
DN_CONV_CH = 2 * DN_QK_W + DN_V_W
MLA_HEADS = 8
MLA_Q_LORA = 384
MLA_KV_LORA = 256
MLA_NOPE = 128
MLA_ROPE = 64
MLA_V = 128
ROPE_BASE = 10000.0
FFN_HIDDEN = -(-8 * D_MODEL // (3 * 256)) * 256
DEPTH = 1
DEEPNORM_ALPHA = (2.0 * DEPTH) ** 0.25
IN_SIZES = (DN_CONV_CH, DN_V_W, DN_HEADS, DN_HEADS, MLA_Q_LORA, MLA_KV_LORA, MLA_ROPE, D_MODEL, D_MODEL)

LANES = 128
SUBLANES = 8
VMEM_LIMIT_BYTES = 56 * 1024 * 1024

LAT_W = 896
LAT_CKV = MLA_Q_LORA
LAT_KR = LAT_CKV + MLA_KV_LORA
LAT_BA = LAT_KR + 2 * MLA_ROPE

BF16 = jnp.bfloat16
F32 = jnp.float32


def _dot(a, b):
    return jnp.dot(a, b, preferred_element_type=F32)


def _dot_nt(a, b):
    return lax.dot_general(a, b, (((1,), (1,)), ((), ())), preferred_element_type=F32)


def _dot_tn(a, b):
    return lax.dot_general(a, b, (((0,), (0,)), ((), ())), preferred_element_type=F32)


def _sigmoid(x):
    return 0.5 * jnp.tanh(0.5 * x) + 0.5


def _silu(x):
    h = 0.5 * x
    return h * jnp.tanh(h) + h


def _layer_norm(r, g, b):
    mu = jnp.mean(r, axis=-1, keepdims=True)
    c = r - mu
    var = jnp.mean(c * c, axis=-1, keepdims=True)
    return c * lax.rsqrt(var + 1e-5) * g + b


def _const_spec(shape):
    return pl.BlockSpec(shape, lambda *_: (0,) * len(shape))


def _params(n_axes, sem=None):
    return pltpu.CompilerParams(
        dimension_semantics=sem or ("arbitrary",) * n_axes, vmem_limit_bytes=VMEM_LIMIT_BYTES)


QKV_TM = 512
QKV_SLAB = 512
QKV_ROWS = 64


def _dn_qkv_kernel(tiles_per_seq, x_ref, w_ref, cw_ref, q_ref, k_ref, v_ref, pre_ref, tail_ref):
    i = pl.program_id(0)
    tm = x_ref.shape[0]
    assert DN_CONV == 4

    @pl.when(i % tiles_per_seq == 0)
    def _():
        pre_ref[:SUBLANES, :] = jnp.zeros((SUBLANES, DN_CONV_CH), F32)
        tail_ref[...] = jnp.zeros_like(tail_ref)

    xb = x_ref[...].astype(BF16)
    outs = (q_ref, k_ref, v_ref)
    slabs_per_out = DN_QK_W // QKV_SLAB
    for j in range(DN_CONV_CH // QKV_SLAB):
        cs = slice(j * QKV_SLAB, (j + 1) * QKV_SLAB)
        pre_ref[SUBLANES:, cs] = _dot(xb, w_ref[:, cs])
        cw = 0.5 * cw_ref[:, cs]
        out_ref = outs[j // slabs_per_out]
        col0 = (j % slabs_per_out) * QKV_SLAB
        which = j // slabs_per_out
        pair_tail = tail_ref[:, cs]
        for r in range(tm // QKV_ROWS):
            ext = pre_ref[r * QKV_ROWS:(r + 1) * QKV_ROWS + SUBLANES, cs]
            pre = ext[SUBLANES:]
            prev = pltpu.roll(ext, 1, axis=0)[SUBLANES:]
            pair = pre * cw[1:2] + prev * cw[0:1]
            pair2 = pltpu.roll(jnp.concatenate([pair_tail, pair], axis=0), 2, axis=0)[SUBLANES:]
            pair_tail = pair[QKV_ROWS - SUBLANES:]
            half_y = pre * cw[3:4] + prev * cw[2:3] + pair2
            y = half_y * jnp.tanh(half_y) + half_y
            rs = slice(r * QKV_ROWS, (r + 1) * QKV_ROWS)
            for hh in range(QKV_SLAB // DN_DK):
                yh = y[:, hh * DN_DK:(hh + 1) * DN_DK]
                if which < 2:
                    ss = jnp.sum(yh * yh, axis=-1, keepdims=True)
                    scale = lax.rsqrt(ss + 1e-6)
                    yh = yh * (scale * (DN_DK ** -0.5) if which == 0 else scale)
                out_ref[rs, col0 + hh * DN_DK:col0 + (hh + 1) * DN_DK] = yh.astype(out_ref.dtype)
        tail_ref[:, cs] = pair_tail
        pre_ref[:SUBLANES, cs] = pre_ref[tm:, cs]


MLA_SCALE = (MLA_NOPE + MLA_ROPE) ** -0.5 * math.log2(math.e)


def _rms(t, w):
    return t * lax.rsqrt(jnp.mean(t * t, axis=-1, keepdims=True) + 1e-6) * w


def _latent_kernel(x_ref, wl_ref, pos_ref, invf_ref, qnw_ref, kvw_ref, wqn_ref, wqr_ref, wuk_ref, wuv_ref,
                   dnp_ref, qn_ref, qr_ref, kn_ref, kr_ref, v_ref, bg_ref):
    xb = x_ref[...].astype(BF16)
    lat = _dot(xb, wl_ref[...])

    tm = x_ref.shape[0]
    qt = tm // 4
    half = MLA_ROPE // 2
    lane = lax.broadcasted_iota(jnp.int32, (1, LANES), 1)
    pos = pos_ref[...].astype(F32)
    grp = lane // half
    pos4 = jnp.where(grp == 0, pos[:qt], jnp.where(grp == 1, pos[qt:2 * qt],
                                                   jnp.where(grp == 2, pos[2 * qt:3 * qt], pos[3 * qt:])))
    ang = pos4 * invf_ref[...]
    cos3 = jnp.concatenate(_split3(jnp.cos(ang)), axis=1)
    sin3 = jnp.concatenate(_split3(jnp.sin(ang)), axis=1)
    src = lax.broadcasted_iota(jnp.int32, (3 * LANES, LANES), 0) % LANES
    dst = lax.broadcasted_iota(jnp.int32, (3 * LANES, LANES), 1)
    cos_parts, sin_parts = [], []
    for g in range(4):
        hit = (src // half == g) & (src % half == dst % half) & (dst < MLA_ROPE)
        cos_parts.append(_dot(cos3, jnp.where(hit, 1.0, 0.0).astype(BF16)))
        sin_parts.append(_dot(sin3, jnp.where(hit, jnp.where(dst < half, -1.0, 1.0), 0.0).astype(BF16)))
    cos_t = jnp.concatenate(cos_parts, axis=0)
    sin_t = jnp.concatenate(sin_parts, axis=0)

    def rope(a):
        return a * cos_t + pltpu.roll(a, MLA_ROPE, axis=1) * sin_t

    c_q = _rms(lat[:, :MLA_Q_LORA], qnw_ref[...] * MLA_SCALE).astype(BF16)
    qn_ref[...] = _dot(c_q, wqn_ref[...]).astype(qn_ref.dtype)
    qr_raw = _dot(c_q, wqr_ref[...])
    for h in range(MLA_HEADS):
        hs = slice(h * LANES, (h + 1) * LANES)
        qr_ref[:, hs] = rope(qr_raw[:, hs]).astype(qr_ref.dtype)

    c_kv = _rms(lat[:, LAT_CKV:LAT_KR], kvw_ref[...]).astype(BF16)
    kn_ref[...] = _dot(c_kv, wuk_ref[...]).astype(kn_ref.dtype)
    v_ref[...] = _dot(c_kv, wuv_ref[...]).astype(v_ref.dtype)
    kr_ref[...] = rope(lat[:, LAT_KR:LAT_BA]).astype(kr_ref.dtype)

    ba = lat[:, LAT_BA:LAT_BA + LANES]
    beta = _sigmoid(ba)
    sp_in = ba + dnp_ref[1:2, :]
    softplus = jnp.maximum(sp_in, 0.0) + jnp.log(1.0 + jnp.exp(-jnp.abs(sp_in)))
    g = -jnp.exp(dnp_ref[0:1, :]) * softplus
    bg_ref[...] = jnp.where(lane < DN_HEADS, beta, jnp.where(lane < 2 * DN_HEADS, g, 0.0))


def _input_kernel(tiles_per_seq, x_ref, wqkv_ref, cw_ref, wl_ref, pos_ref, invf_ref, qnw_ref, kvw_ref, wqn_ref,
                  wqr_ref, wuk_ref, wuv_ref, dnp_ref, q_ref, k_ref, v_ref, qn_ref, qr_ref, kn_ref, kr_ref, vm_ref,
                  bg_ref, pre_ref, tail_ref):
    _dn_qkv_kernel(tiles_per_seq, x_ref, wqkv_ref, cw_ref, q_ref, k_ref, v_ref, pre_ref, tail_ref)
    _latent_kernel(x_ref, wl_ref, pos_ref, invf_ref, qnw_ref, kvw_ref, wqn_ref, wqr_ref, wuk_ref, wuv_ref, dnp_ref,
                   qn_ref, qr_ref, kn_ref, kr_ref, vm_ref, bg_ref)


def _input_stage(x2, w_in_bf, conv_w, w_lat, pos2, invf, qnw, kvw, wqn, wqr, wuk, wuv, dnp, seq):
    t = x2.shape[0]
    tm = QKV_TM
    row = lambda i: (i, 0)
    dn_out = jax.ShapeDtypeStruct((t, DN_QK_W), BF16)
    wide = jax.ShapeDtypeStruct((t, MLA_HEADS * LANES), BF16)
    wide_spec = pl.BlockSpec((tm, MLA_HEADS * LANES), row)
    lane_spec = pl.BlockSpec((tm, LANES), row)
    consts = [conv_w, w_lat]
    consts2 = [invf, qnw, kvw, wqn, wqr, wuk, wuv, dnp]
    return pl.pallas_call(
        functools.partial(_input_kernel, seq // tm),
        grid=(t // tm,),
        in_specs=[pl.BlockSpec((tm, D_MODEL), row), _const_spec((D_MODEL, DN_CONV_CH))]
        + [_const_spec(a.shape) for a in consts] + [pl.BlockSpec((tm, 1), row)]
        + [_const_spec(a.shape) for a in consts2],
        out_specs=[pl.BlockSpec((tm, DN_QK_W), row)] * 3 + [wide_spec] * 3 + [lane_spec, wide_spec, lane_spec],
        out_shape=[dn_out] * 3 + [wide, wide, wide, jax.ShapeDtypeStruct((t, LANES), BF16), wide,
                                  jax.ShapeDtypeStruct((t, LANES), F32)],
        scratch_shapes=[pltpu.VMEM((SUBLANES + tm, DN_CONV_CH), F32), pltpu.VMEM((SUBLANES, DN_CONV_CH), F32)],
        compiler_params=_params(1),
        name="input_stage",
    )(x2, w_in_bf, conv_w, w_lat, pos2, invf, qnw, kvw, wqn, wqr, wuk, wuv, dnp)


PREP_ROWS = 1024
PREP_HEADS = 8
C = DN_CHUNK


def _split3(a):
    a1 = a.astype(BF16)
    r1 = a - a1.astype(F32)
    a2 = r1.astype(BF16)
    a3 = (r1 - a2.astype(F32)).astype(BF16)
    return a1, a2, a3


def _pair_blockdiag(m):
    lane = lax.broadcasted_iota(jnp.int32, m.shape, 1)
    zero = jnp.zeros_like(m)
    return jnp.concatenate([jnp.where(lane < C, m, zero), jnp.where(lane >= C, m, zero)], axis=0)


def _pair_mm(a, b):
    return _dot(a.astype(BF16), _pair_blockdiag(b.astype(BF16)))


def _dn_prep_kernel(q_ref, k_ref, v_ref, bg_ref, w_ref, qd_ref, kt_ref, u_ref, intra_ref, egl_ref):
    assert PREP_HEADS == DN_HEADS
    rows = q_ref.shape[0]
    nc = rows // C
    pc = 2 * C

    row = lax.broadcasted_iota(jnp.int32, (C, pc), 0)
    lane = lax.broadcasted_iota(jnp.int32, (C, pc), 1)
    col = lane % C
    is_a = lane < C
    tril = row >= col
    strict = row > col
    eye = (row == col).astype(F32)
    tril3 = (lax.broadcasted_iota(jnp.int32, (C, 3 * C), 1) % C
             <= lax.broadcasted_iota(jnp.int32, (C, 3 * C), 0)).astype(BF16)

    bg = bg_ref[...]
    betas = [jnp.broadcast_to(bg[:, hh:hh + 1], (rows, LANES)) for hh in range(DN_HEADS)]

    chunks = [slice(c * C, (c + 1) * C) for c in range(nc)]
    units = [(hh, p) for hh in range(PREP_HEADS) for p in range(nc // 2)]
    cums = [_dot(tril3, jnp.concatenate(_split3(bg[rs, :]), axis=0)) for rs in chunks]
    gcs = [[jnp.broadcast_to(cum[:, DN_HEADS + hh:DN_HEADS + hh + 1], (C, LANES)) for cum in cums]
           for hh in range(DN_HEADS)]
    pick_t = (lax.broadcasted_iota(jnp.int32, (2 * SUBLANES, 3 * LANES), 1) % LANES
              == lax.broadcasted_iota(jnp.int32, (2 * SUBLANES, 3 * LANES), 0) + DN_HEADS).astype(BF16)
    cum_t = [_dot_nt(pick_t, jnp.concatenate([jnp.concatenate(_split3(cums[2 * p]), axis=1),
                                              jnp.concatenate(_split3(cums[2 * p + 1]), axis=1)], axis=0))
             for p in range(nc // 2)]
    rowforms = [jnp.broadcast_to(cum_t[p][hh:hh + 1, :], (C, pc)) for hh, p in units]
    decays, kbs, l_mats, qks = [], [], [], []
    for (hh, p), rowform in zip(units, rowforms):
        rs, hs = slice(p * pc, (p + 1) * pc), slice(hh * DN_DK, (hh + 1) * DN_DK)
        diff = jnp.where(is_a, gcs[hh][2 * p], gcs[hh][2 * p + 1]) - rowform
        decay = jnp.where(tril, jnp.exp(jnp.where(tril, diff, 0.0)), 0.0)
        k = k_ref[rs, hs]
        kb = k * betas[hh][rs, :]
        kbb, qb = kb.astype(BF16), q_ref[rs, hs].astype(BF16)
        res = _dot_nt(jnp.concatenate([kbb[:C], qb[:C], kbb[C:], qb[C:]], axis=0), k.astype(BF16))
        decays.append(decay)
        kbs.append(kb)
        l_mats.append(jnp.where(strict, jnp.where(is_a, res[:C], res[2 * C:3 * C]) * decay, 0.0))
        qks.append(jnp.where(is_a, res[C:2 * C], res[3 * C:]))

    invs = [eye] * len(units)
    s = 1
    while s < C:
        blk = ((row // (2 * s)) == (col // (2 * s))) & ((row % (2 * s)) >= s) & ((col % (2 * s)) < s)
        offs = [jnp.where(blk, l, 0.0) for l in l_mats]
        if s == 1:
            invs = [inv - off for inv, off in zip(invs, offs)]
        else:
            mids = [_pair_mm(off, inv) for off, inv in zip(offs, invs)]
            invs = [inv - _pair_mm(inv, mid) for inv, mid in zip(invs, mids)]
        s *= 2

    for n, (hh, p) in enumerate(units):
        rs, hs = slice(p * pc, (p + 1) * pc), slice(hh * DN_DK, (hh + 1) * DN_DK)
        ga, gb = gcs[hh][2 * p], gcs[hh][2 * p + 1]
        egc = jnp.exp(jnp.concatenate([ga, gb], axis=0))
        rhs = jnp.concatenate([(v_ref[rs, hs] * betas[hh][rs, :]).astype(BF16), (kbs[n] * egc).astype(BF16)], axis=1)
        uw = _dot(_pair_blockdiag(invs[n].astype(BF16)), rhs)
        u_ref[rs, hs] = uw[:, :DN_DV].astype(u_ref.dtype)
        w_ref[rs, hs] = uw[:, DN_DV:].astype(w_ref.dtype)
        intra_ref[hh, p * C:(p + 1) * C, :] = jnp.where(tril, qks[n] * decays[n], 0.0).astype(intra_ref.dtype)
        qd_ref[rs, hs] = (q_ref[rs, hs] * egc).astype(qd_ref.dtype)
        gla, glb = ga[C - 1:C, :], gb[C - 1:C, :]
        tail = jnp.exp(jnp.concatenate([gla - ga, glb - gb], axis=0))
        kt_ref[rs, hs] = (k_ref[rs, hs] * tail).astype(kt_ref.dtype)
        egl_ref[hh, 2 * p:2 * p + 1, :] = jnp.exp(gla)
        egl_ref[hh, 2 * p + 1:2 * p + 2, :] = jnp.exp(glb)


def _dn_prep(q, k, v, bg):
    t = q.shape[0]
    rows = PREP_ROWS
    blk = pl.BlockSpec((rows, PREP_HEADS * DN_DK), lambda i, h: (i, h))
    wide_bf = jax.ShapeDtypeStruct((t, DN_QK_W), BF16)
    return pl.pallas_call(
        _dn_prep_kernel,
        grid=(t // rows, DN_HEADS // PREP_HEADS),
        in_specs=[blk, blk, blk, pl.BlockSpec((rows, LANES), lambda i, h: (i, 0))],
        out_specs=[blk, blk, blk, blk, pl.BlockSpec((PREP_HEADS, rows // 2, 2 * C), lambda i, h: (h, i, 0)),
                   pl.BlockSpec((PREP_HEADS, rows // C, LANES), lambda i, h: (h, i, 0))],
        out_shape=[wide_bf, wide_bf, wide_bf, wide_bf,
                   jax.ShapeDtypeStruct((DN_HEADS, t // 2, 2 * C), BF16),
                   jax.ShapeDtypeStruct((DN_HEADS, t // C, LANES), F32)],
        compiler_params=_params(2),
        name="dn_prep",
    )(q, k, v, bg)


SCAN_ROWS = 512
SCAN_BATCH = 4


def _dn_scan_kernel(w_ref, qd_ref, kt_ref, u_ref, intra_ref, egl_ref, o_ref, state_ref):
    @pl.when(pl.program_id(1) == 0)
    def _():
        state_ref[...] = jnp.zeros_like(state_ref)

    nb, rows = w_ref.shape[0], w_ref.shape[1]
    chains = [(bb, h, slice(h * DN_DK, (h + 1) * DN_DK)) for bb in range(nb) for h in range(DN_HEADS)]
    states = [state_ref[bb, h] for bb, h, _ in chains]
    zeros = jnp.zeros((C, DN_DV), BF16)
    for c in range(rows // C):
        rs = slice(c * C, (c + 1) * C)
        r1s = [_dot(jnp.concatenate([w_ref[bb, rs, hs], qd_ref[bb, rs, hs]], axis=0), st.astype(BF16))
               for (bb, _, hs), st in zip(chains, states)]
        vbs = [(u_ref[bb, rs, hs] - r1[:C]).astype(BF16) for (bb, _, hs), r1 in zip(chains, r1s)]
        ps = slice((c // 2) * C, (c // 2 + 1) * C)
        vps = [jnp.concatenate([vb, zeros] if c % 2 == 0 else [zeros, vb], axis=0) for vb in vbs]
        outs = [r1[C:] + _dot(intra_ref[h, bb, ps, :], vp) for (bb, h, _), r1, vp in zip(chains, r1s, vps)]
        upds = [_dot_tn(kt_ref[bb, rs, hs], vb) for (bb, _, hs), vb in zip(chains, vbs)]
        for (bb, _, hs), out in zip(chains, outs):
            o_ref[bb, rs, hs] = out.astype(o_ref.dtype)
        states = [st * egl_ref[h, bb, c:c + 1, :] + upd for (bb, h, _), st, upd in zip(chains, states, upds)]
    for (bb, h, _), st in zip(chains, states):
        state_ref[bb, h] = st


def _dn_scan(w, qd, kt, u, intra, egl, batch, seq):
    rows = SCAN_ROWS
    nb = SCAN_BATCH
    as_bs = lambda a: a.reshape(batch, seq, a.shape[-1])
    wide = pl.BlockSpec((nb, rows, DN_QK_W), lambda b, i: (b, i, 0))
    intra4 = intra.reshape(DN_HEADS, batch, seq // 2, 2 * C)
    egl4 = egl.reshape(DN_HEADS, batch, seq // C, LANES)
    out = pl.pallas_call(
        _dn_scan_kernel,
        grid=(batch // nb, seq // rows),
        in_specs=[wide, wide, wide, wide,
                  pl.BlockSpec((DN_HEADS, nb, rows // 2, 2 * C), lambda b, i: (0, b, i, 0)),
                  pl.BlockSpec((DN_HEADS, nb, rows // C, LANES), lambda b, i: (0, b, i, 0))],
        out_specs=wide,
        out_shape=jax.ShapeDtypeStruct((batch, seq, DN_V_W), BF16),
        scratch_shapes=[pltpu.VMEM((nb, DN_HEADS, DN_DK, DN_DV), F32)],
        compiler_params=_params(2),
        name="dn_scan",
    )(as_bs(w), as_bs(qd), as_bs(kt), as_bs(u), intra4, egl4)
    return out.reshape(batch * seq, DN_V_W)


ATT_TQ = 512
ATT_TK = 512
NEG_INF = -1e30


def _mla_attn_kernel(qn_ref, qr_ref, kn_ref, kr_ref, v_ref, o_ref, qcat_ref, kcat_ref, vcat_ref):
    seq = qn_ref.shape[0]
    tq, tk = ATT_TQ, ATT_TK
    qcat_ref[:, :LANES] = qn_ref[...]
    qcat_ref[:, LANES:] = qr_ref[...]
    kcat_ref[:, :LANES] = kn_ref[...]
    kcat_ref[:, LANES:] = kr_ref[...]
    vcat_ref[:, :LANES] = v_ref[...]
    vcat_ref[:, LANES:] = jnp.ones((seq, LANES), BF16)
    hq = tq // 2
    causal = (lax.broadcasted_iota(jnp.int32, (hq, hq), 1) <= lax.broadcasted_iota(jnp.int32, (hq, hq), 0))
    masked = jnp.full((hq, hq), NEG_INF, F32)

    for i in range(seq // tq):
        q = qcat_ref[i * tq:(i + 1) * tq, :]
        m = acc = None
        for j in range(i + 1):
            k0 = j * tk
            if j < i:
                s = _dot_nt(q, kcat_ref[k0:k0 + tk, :])
            else:
                left = _dot_nt(q, kcat_ref[k0:k0 + hq, :])
                left = jnp.concatenate([jnp.where(causal, left[:hq], NEG_INF), left[hq:]], axis=0)
                right = jnp.where(causal, _dot_nt(q[hq:], kcat_ref[k0 + hq:k0 + tk, :]), NEG_INF)
                s = jnp.concatenate([left, jnp.concatenate([masked, right], axis=0)], axis=1)
            m_cur = jnp.max(s, axis=1, keepdims=True)
            if j == 0:
                m_new = jnp.broadcast_to(m_cur, (tq, LANES))
            else:
                m_new = jnp.maximum(m, m_cur)
            p = jnp.exp2(s - jnp.concatenate([m_new] * (tk // LANES), axis=1)).astype(BF16)
            if j < i:
                pv = _dot(p, vcat_ref[k0:k0 + tk, :])
            else:
                pv = _dot(p[:, :hq], vcat_ref[k0:k0 + hq, :])
                pv = pv + jnp.concatenate([jnp.zeros((hq, 2 * LANES), F32),
                                           _dot(p[hq:, hq:], vcat_ref[k0 + hq:k0 + tk, :])], axis=0)
            if j == 0:
                acc = pv
            else:
                alpha = jnp.exp2(m - m_new)
                acc = jnp.concatenate([alpha, alpha], axis=1) * acc + pv
            m = m_new
        o_ref[i * tq:(i + 1) * tq, :] = (acc[:, :LANES] / acc[:, LANES:]).astype(o_ref.dtype)


def _mla_attn(qn, qr, kn, kr, v, batch, seq):
    t = qn.shape[0]
    blk = pl.BlockSpec((seq, LANES), lambda b, h: (b, h))
    return pl.pallas_call(
        _mla_attn_kernel,
        grid=(batch, MLA_HEADS),
        in_specs=[blk, blk, blk, pl.BlockSpec((seq, LANES), lambda b, h: (b, 0)), blk],
        out_specs=blk,
        out_shape=jax.ShapeDtypeStruct((t, MLA_HEADS * MLA_V), BF16),
        scratch_shapes=[pltpu.VMEM((seq, 2 * LANES), BF16)] * 3,
        compiler_params=_params(2),
        name="mla_attn",
    )(qn, qr, kn, kr, v)


MERGE_TM = 512
MERGE_ROWS = 256


def _merge_kernel(x_ref, odn_ref, omla_ref, wzg_ref, wdn_ref, wmla_ref, wo_ref, nw_ref, g_ref, b_ref, h_ref):
    tm = x_ref.shape[0]
    groups = [slice(r * MERGE_ROWS, (r + 1) * MERGE_ROWS) for r in range(tm // MERGE_ROWS)]
    xs = [x_ref[rs, :] for rs in groups]
    zgs = [_dot(x.astype(BF16), wzg_ref[...]) for x in xs]
    y_mlas = [_dot(omla_ref[rs, :], wmla_ref[...]) for rs in groups]
    o_dns = []
    for rs, zg in zip(groups, zgs):
        parts = [_rms(odn_ref[rs, h * DN_DV:(h + 1) * DN_DV].astype(F32), nw_ref[...]) for h in range(DN_HEADS)]
        o_dns.append((jnp.concatenate(parts, axis=1) * _silu(zg[:, :DN_V_W])).astype(BF16))
    y_dns = [_dot(o_dn, wdn_ref[...]) for o_dn in o_dns]
    mixeds = [(_sigmoid(zg[:, DN_V_W:DN_V_W + D_MODEL]) * y_dn + _sigmoid(zg[:, DN_V_W + D_MODEL:]) * y_mla).astype(BF16)
              for zg, y_dn, y_mla in zip(zgs, y_dns, y_mlas)]
    outs = [_dot(mixed, wo_ref[...]) for mixed in mixeds]
    for rs, x, out in zip(groups, xs, outs):
        h_ref[rs, :] = _layer_norm(DEEPNORM_ALPHA * x + out, g_ref[...], b_ref[...])


def _merge(x2, o_dn, o_mla, wzg, wdn, wmla, wo, nw, g, b):
    t = x2.shape[0]
    tm = MERGE_TM
    row = lambda i: (i, 0)
    act = pl.BlockSpec((tm, D_MODEL), row)
    return pl.pallas_call(
        _merge_kernel,
        grid=(t // tm,),
        in_specs=[act, act, act, _const_spec(wzg.shape), _const_spec(wdn.shape), _const_spec(wmla.shape),
                  _const_spec(wo.shape), _const_spec(nw.shape), _const_spec(g.shape), _const_spec(b.shape)],
        out_specs=act,
        out_shape=jax.ShapeDtypeStruct((t, D_MODEL), F32),
        compiler_params=_params(1),
        name="merge",
    )(x2, o_dn, o_mla, wzg, wdn, wmla, wo, nw, g, b)


FFN_TM = 512
FFN_SLAB = 256


def _ffn_kernel(h_ref, p_ref, win_ref, wout_ref, wpg_ref, wple_ref, g_ref, b_ref, o_ref):
    h = h_ref[...]
    hb = h.astype(BF16)
    acc = jnp.zeros(h.shape, F32)
    for j in range(FFN_HIDDEN // FFN_SLAB):
        cs = slice(j * FFN_SLAB, (j + 1) * FFN_SLAB)
        gt = _dot(hb, win_ref[:, cs])
        up = _dot(hb, win_ref[:, FFN_HIDDEN + j * FFN_SLAB:FFN_HIDDEN + (j + 1) * FFN_SLAB])
        a = (_silu(gt) * up).astype(BF16)
        acc = acc + _dot(a, wout_ref[cs, :])
    ple = _sigmoid(_dot(hb, wpg_ref[...])) * _dot(p_ref[...].astype(BF16), wple_ref[...])
    r = DEEPNORM_ALPHA * h + acc + ple
    o_ref[...] = _layer_norm(r, g_ref[...], b_ref[...])


def _ffn(h1, p2, win, wout, wpg, wple, g, b):
    assert FFN_HIDDEN % FFN_SLAB == 0
    t = h1.shape[0]
    tm = FFN_TM
    row = lambda i: (i, 0)
    act = pl.BlockSpec((tm, D_MODEL), row)
    return pl.pallas_call(
        _ffn_kernel,
        grid=(t // tm,),
        in_specs=[act, pl.BlockSpec((tm, PLE_DIM), row), _const_spec(win.shape), _const_spec(wout.shape),
                  _const_spec(wpg.shape), _const_spec(wple.shape), _const_spec(g.shape), _const_spec(b.shape)],
        out_specs=act,
        out_shape=jax.ShapeDtypeStruct((t, D_MODEL), F32),
        compiler_params=_params(1),
        name="ffn",
    )(h1, p2, win, wout, wpg, wple, g, b)


def _split_w_in(w_in):
    offs = [0]
    for n in IN_SIZES:
        offs.append(offs[-1] + n)
    return [w_in[:, offs[i]:offs[i + 1]] for i in range(len(IN_SIZES))]


def kernel(x, p, positions, w_in, conv_w, dn_a_log, dn_dt_bias, dn_norm_w, q_norm_w, w_uq, kv_norm_w, w_uk, w_uv,
           w_br_dn, w_br_mla, w_o, ln1_g, ln1_b, w_ffn_in, w_ffn_out, w_ple, w_ple_gate, ln2_g, ln2_b):
    assert w_in.shape[0] == DEPTH
    batch, seq, _ = x.shape
    t = batch * seq
    assert x.shape[2] == D_MODEL and batch % SCAN_BATCH == 0
    assert all(seq % n == 0 for n in (QKV_TM, PREP_ROWS, SCAN_ROWS, ATT_TQ, ATT_TK)) and ATT_TQ == ATT_TK
    assert t % MERGE_TM == 0 and t % FFN_TM == 0 and MERGE_TM % MERGE_ROWS == 0
    h = x.reshape(t, D_MODEL)
    pos2 = positions.reshape(t, 1)
    half = MLA_ROPE // 2
    invf = ROPE_BASE ** (-jnp.arange(0, MLA_ROPE, 2, dtype=F32) / MLA_ROPE)
    invf = jnp.tile(invf, LANES // half).reshape(1, LANES)

    for i in range(DEPTH):
        w_in_bf = w_in[i].astype(BF16)
        _, w_z, w_b, w_a, w_cq, w_ckv, w_kr, w_gdn, w_gmla = _split_w_in(w_in_bf)
        w_kr_sw = jnp.concatenate([w_kr[:, half:], w_kr[:, :half]], axis=1)
        pad = jnp.zeros((D_MODEL, LAT_W - LAT_BA - 2 * DN_HEADS), BF16)
        w_lat = jnp.concatenate([w_cq, w_ckv, w_kr, w_kr_sw, w_b, w_a, pad], axis=1)
        w_zg = jnp.concatenate([w_z, w_gdn, w_gmla], axis=1)
        uq = w_uq[i]
        t1 = uq[:, :, MLA_NOPE:MLA_NOPE + half]
        t2 = uq[:, :, MLA_NOPE + half:]
        wqn = uq[:, :, :MLA_NOPE].reshape(MLA_Q_LORA, MLA_HEADS * MLA_NOPE).astype(BF16)
        wqr = jnp.concatenate([t1, t2, t2, t1], axis=2).reshape(MLA_Q_LORA, MLA_HEADS * LANES).astype(BF16)
        wuk = w_uk[i].reshape(MLA_KV_LORA, MLA_HEADS * MLA_NOPE).astype(BF16)
        wuv = w_uv[i].reshape(MLA_KV_LORA, MLA_HEADS * MLA_V).astype(BF16)
        lane_pad = jnp.zeros((LANES - 2 * DN_HEADS,), F32)
        head_pad = jnp.zeros((DN_HEADS,), F32)
        dnp = jnp.stack([jnp.concatenate([head_pad, dn_a_log[i], lane_pad]),
                         jnp.concatenate([head_pad, dn_dt_bias[i], lane_pad])])
        nw = jnp.tile(dn_norm_w[i], 1).reshape(1, DN_DV)
        w_ffn = w_ffn_in[i].astype(BF16)

        q, k, v, qn, qr, kn, kr, vm, bg = _input_stage(
            h, w_in_bf, conv_w[i], w_lat, pos2, invf, q_norm_w[i].reshape(1, -1), kv_norm_w[i].reshape(1, -1),
            wqn, wqr, wuk, wuv, dnp, seq)
        w, qd, kt, u, intra, egl = _dn_prep(q, k, v, bg)
        o_dn = _dn_scan(w, qd, kt, u, intra, egl, batch, seq)
        o_mla = _mla_attn(qn, qr, kn, kr, vm, batch, seq)
        h1 = _merge(h, o_dn, o_mla, w_zg, w_br_dn[i].astype(BF16), w_br_mla[i].astype(BF16),
                    w_o[i].astype(BF16), nw, ln1_g[i].reshape(1, -1), ln1_b[i].reshape(1, -1))
        h = _ffn(h1, p[i].reshape(t, PLE_DIM), w_ffn,
                 w_ffn_out[i].astype(BF16), w_ple_gate[i].astype(BF16), w_ple[i].astype(BF16),
                 ln2_g[i].reshape(1, -1), ln2_b[i].reshape(1, -1))
    return h.reshape(batch, seq, D_MODEL)
```

```python
import functools
import math

import jax
import jax.numpy as jnp
from jax import lax
from jax.experimental import pallas as pl
from jax.experimental.pallas import tpu as pltpu

D_MODEL = 1024
PLE_DIM = 256
DN_HEADS = 8
DN_DK = 128
DN_DV = 128
DN_CONV = 4
DN_CHUNK = 64
DN_QK_W = DN_HEADS * DN_DK
DN_V_W = DN_HEADS * DN_DV
DN_CONV_CH = 2 * DN_QK_W + DN_V_W
MLA_HEADS = 8
MLA_Q_LORA = 384
MLA_KV_LORA = 256
MLA_NOPE = 128
MLA_ROPE = 64
MLA_V = 128
ROPE_BASE = 10000.0
FFN_HIDDEN = -(-8 * D_MODEL // (3 * 256)) * 256
DEPTH = 1
DEEPNORM_ALPHA = (2.0 * DEPTH) ** 0.25
IN_SIZES = (DN_CONV_CH, DN_V_W, DN_HEADS, DN_HEADS, MLA_Q_LORA, MLA_KV_LORA, MLA_ROPE, D_MODEL, D_MODEL)

LANES = 128
SUBLANES = 8
VMEM_LIMIT_BYTES = 56 * 1024 * 1024

LAT_W = 896
LAT_CKV = MLA_Q_LORA
LAT_KR = LAT_CKV + MLA_KV_LORA
LAT_BA = LAT_KR + 2 * MLA_ROPE

BF16 = jnp.bfloat16
F32 = jnp.float32


def _dot(a, b):
    return jnp.dot(a, b, preferred_element_type=F32)


def _dot_nt(a, b):
    return lax.dot_general(a, b, (((1,), (1,)), ((), ())), preferred_element_type=F32)


def _dot_tn(a, b):
    return lax.dot_general(a, b, (((0,), (0,)), ((), ())), preferred_element_type=F32)


def _sigmoid(x):
    return 0.5 * jnp.tanh(0.5 * x) + 0.5


def _silu(x):
    h = 0.5 * x
    return h * jnp.tanh(h) + h


def _layer_norm(r, g, b):
    mu = jnp.mean(r, axis=-1, keepdims=True)
    c = r - mu
    var = jnp.mean(c * c, axis=-1, keepdims=True)
    return c * lax.rsqrt(var + 1e-5) * g + b


def _const_spec(shape):
    return pl.BlockSpec(shape, lambda *_: (0,) * len(shape))


def _params(n_axes, sem=None):
    return pltpu.CompilerParams(
        dimension_semantics=sem or ("arbitrary",) * n_axes, vmem_limit_bytes=VMEM_LIMIT_BYTES)


QKV_TM = 512
QKV_SLAB = 512
QKV_ROWS = 64


def _dn_qkv_kernel(tiles_per_seq, x_ref, w_ref, cw_ref, q_ref, k_ref, v_ref, pre_ref, tail_ref):
    i = pl.program_id(0)
    tm = x_ref.shape[0]
    assert DN_CONV == 4

    @pl.when(i % tiles_per_seq == 0)
    def _():
        pre_ref[:SUBLANES, :] = jnp.zeros((SUBLANES, DN_CONV_CH), F32)
        tail_ref[...] = jnp.zeros_like(tail_ref)

    xb = x_ref[...].astype(BF16)
    outs = (q_ref, k_ref, v_ref)
    slabs_per_out = DN_QK_W // QKV_SLAB
    for j in range(DN_CONV_CH // QKV_SLAB):
        cs = slice(j * QKV_SLAB, (j + 1) * QKV_SLAB)
        pre_ref[SUBLANES:, cs] = _dot(xb, w_ref[:, cs])
        cw = 0.5 * cw_ref[:, cs]
        out_ref = outs[j // slabs_per_out]
        col0 = (j % slabs_per_out) * QKV_SLAB
        which = j // slabs_per_out
        pair_tail = tail_ref[:, cs]
        for r in range(tm // QKV_ROWS):
            ext = pre_ref[r * QKV_ROWS:(r + 1) * QKV_ROWS + SUBLANES, cs]
            pre = ext[SUBLANES:]
            prev = pltpu.roll(ext, 1, axis=0)[SUBLANES:]
            pair = pre * cw[1:2] + prev * cw[0:1]
            pair2 = pltpu.roll(jnp.concatenate([pair_tail, pair], axis=0), 2, axis=0)[SUBLANES:]
            pair_tail = pair[QKV_ROWS - SUBLANES:]
            half_y = pre * cw[3:4] + prev * cw[2:3] + pair2
            y = half_y * jnp.tanh(half_y) + half_y
            rs = slice(r * QKV_ROWS, (r + 1) * QKV_ROWS)
            for hh in range(QKV_SLAB // DN_DK):
                yh = y[:, hh * DN_DK:(hh + 1) * DN_DK]
                if which < 2:
                    ss = jnp.sum(yh * yh, axis=-1, keepdims=True)
                    scale = lax.rsqrt(ss + 1e-6)
                    yh = yh * (scale * (DN_DK ** -0.5) if which == 0 else scale)
                out_ref[rs, col0 + hh * DN_DK:col0 + (hh + 1) * DN_DK] = yh.astype(out_ref.dtype)
        tail_ref[:, cs] = pair_tail
        pre_ref[:SUBLANES, cs] = pre_ref[tm:, cs]


MLA_SCALE = (MLA_NOPE + MLA_ROPE) ** -0.5 * math.log2(math.e)


def _rms(t, w):
    return t * lax.rsqrt(jnp.mean(t * t, axis=-1, keepdims=True) + 1e-6) * w


def _latent_kernel(x_ref, wl_ref, pos_ref, invf_ref, qnw_ref, kvw_ref, wqn_ref, wqr_ref, wuk_ref, wuv_ref,
                   dnp_ref, qn_ref, qr_ref, kn_ref, kr_ref, v_ref, bg_ref):
    xb = x_ref[...].astype(BF16)
    lat = _dot(xb, wl_ref[...])

    tm = x_ref.shape[0]
    qt = tm // 4
    half = MLA_ROPE // 2
    lane = lax.broadcasted_iota(jnp.int32, (1, LANES), 1)
    pos = pos_ref[...].astype(F32)
    grp = lane // half
    pos4 = jnp.where(grp == 0, pos[:qt], jnp.where(grp == 1, pos[qt:2 * qt],
                                                   jnp.where(grp == 2, pos[2 * qt:3 * qt], pos[3 * qt:])))
    ang = pos4 * invf_ref[...]
    cos3 = jnp.concatenate(_split3(jnp.cos(ang)), axis=1)
    sin3 = jnp.concatenate(_split3(jnp.sin(ang)), axis=1)
    src = lax.broadcasted_iota(jnp.int32, (3 * LANES, LANES), 0) % LANES
    dst = lax.broadcasted_iota(jnp.int32, (3 * LANES, LANES), 1)
    cos_parts, sin_parts = [], []
    for g in range(4):
        hit = (src // half == g) & (src % half == dst % half) & (dst < MLA_ROPE)
        cos_parts.append(_dot(cos3, jnp.where(hit, 1.0, 0.0).astype(BF16)))
        sin_parts.append(_dot(sin3, jnp.where(hit, jnp.where(dst < half, -1.0, 1.0), 0.0).astype(BF16)))
    cos_t = jnp.concatenate(cos_parts, axis=0)
    sin_t = jnp.concatenate(sin_parts, axis=0)

    def rope(a):
        return a * cos_t + pltpu.roll(a, MLA_ROPE, axis=1) * sin_t

    c_q = _rms(lat[:, :MLA_Q_LORA], qnw_ref[...] * MLA_SCALE).astype(BF16)
    qn_ref[...] = _dot(c_q, wqn_ref[...]).astype(qn_ref.dtype)
    qr_raw = _dot(c_q, wqr_ref[...])
    for h in range(MLA_HEADS):
        hs = slice(h * LANES, (h + 1) * LANES)
        qr_ref[:, hs] = rope(qr_raw[:, hs]).astype(qr_ref.dtype)

    c_kv = _rms(lat[:, LAT_CKV:LAT_KR], kvw_ref[...]).astype(BF16)
    kn_ref[...] = _dot(c_kv, wuk_ref[...]).astype(kn_ref.dtype)
    v_ref[...] = _dot(c_kv, wuv_ref[...]).astype(v_ref.dtype)
    kr_ref[...] = rope(lat[:, LAT_KR:LAT_BA]).astype(kr_ref.dtype)

    ba = lat[:, LAT_BA:LAT_BA + LANES]
    beta = _sigmoid(ba)
    sp_in = ba + dnp_ref[1:2, :]
    softplus = jnp.maximum(sp_in, 0.0) + jnp.log(1.0 + jnp.exp(-jnp.abs(sp_in)))
    g = -jnp.exp(dnp_ref[0:1, :]) * softplus
    bg_ref[...] = jnp.where(lane < DN_HEADS, beta, jnp.where(lane < 2 * DN_HEADS, g, 0.0))


def _input_kernel(tiles_per_seq, x_ref, wqkv_ref, cw_ref, wl_ref, pos_ref, invf_ref, qnw_ref, kvw_ref, wqn_ref,
                  wqr_ref, wuk_ref, wuv_ref, dnp_ref, q_ref, k_ref, v_ref, qn_ref, qr_ref, kn_ref, kr_ref, vm_ref,
                  bg_ref, pre_ref, tail_ref):
    _dn_qkv_kernel(tiles_per_seq, x_ref, wqkv_ref, cw_ref, q_ref, k_ref, v_ref, pre_ref, tail_ref)
    _latent_kernel(x_ref, wl_ref, pos_ref, invf_ref, qnw_ref, kvw_ref, wqn_ref, wqr_ref, wuk_ref, wuv_ref, dnp_ref,
                   qn_ref, qr_ref, kn_ref, kr_ref, vm_ref, bg_ref)


def _input_stage(x2, w_in_bf, conv_w, w_lat, pos2, invf, qnw, kvw, wqn, wqr, wuk, wuv, dnp, seq):
    t = x2.shape[0]
    tm = QKV_TM
    row = lambda i: (i, 0)
    dn_out = jax.ShapeDtypeStruct((t, DN_QK_W), BF16)
    wide = jax.ShapeDtypeStruct((t, MLA_HEADS * LANES), BF16)
    wide_spec = pl.BlockSpec((tm, MLA_HEADS * LANES), row)
    lane_spec = pl.BlockSpec((tm, LANES), row)
    consts = [conv_w, w_lat]
    consts2 = [invf, qnw, kvw, wqn, wqr, wuk, wuv, dnp]
    return pl.pallas_call(
        functools.partial(_input_kernel, seq // tm),
        grid=(t // tm,),
        in_specs=[pl.BlockSpec((tm, D_MODEL), row), _const_spec((D_MODEL, DN_CONV_CH))]
        + [_const_spec(a.shape) for a in consts] + [pl.BlockSpec((tm, 1), row)]
        + [_const_spec(a.shape) for a in consts2],
        out_specs=[pl.BlockSpec((tm, DN_QK_W), row)] * 3 + [wide_spec] * 3 + [lane_spec, wide_spec, lane_spec],
        out_shape=[dn_out] * 3 + [wide, wide, wide, jax.ShapeDtypeStruct((t, LANES), BF16), wide,
                                  jax.ShapeDtypeStruct((t, LANES), F32)],
        scratch_shapes=[pltpu.VMEM((SUBLANES + tm, DN_CONV_CH), F32), pltpu.VMEM((SUBLANES, DN_CONV_CH), F32)],
        compiler_params=_params(1),
        name="input_stage",
    )(x2, w_in_bf, conv_w, w_lat, pos2, invf, qnw, kvw, wqn, wqr, wuk, wuv, dnp)


PREP_ROWS = 1024
PREP_HEADS = 8
C = DN_CHUNK


def _split3(a):
    a1 = a.astype(BF16)
    r1 = a - a1.astype(F32)
    a2 = r1.astype(BF16)
    a3 = (r1 - a2.astype(F32)).astype(BF16)
    return a1, a2, a3


def _pair_blockdiag(m):
    lane = lax.broadcasted_iota(jnp.int32, m.shape, 1)
    zero = jnp.zeros_like(m)
    return jnp.concatenate([jnp.where(lane < C, m, zero), jnp.where(lane >= C, m, zero)], axis=0)


def _pair_mm(a, b):
    return _dot(a.astype(BF16), _pair_blockdiag(b.astype(BF16)))


def _dn_prep_kernel(q_ref, k_ref, v_ref, bg_ref, w_ref, qd_ref, kt_ref, u_ref, intra_ref, egl_ref):
    assert PREP_HEADS == DN_HEADS
    rows = q_ref.shape[0]
    nc = rows // C
    pc = 2 * C

    row = lax.broadcasted_iota(jnp.int32, (C, pc), 0)
    lane = lax.broadcasted_iota(jnp.int32, (C, pc), 1)
    col = lane % C
    is_a = lane < C
    tril = row >= col
    strict = row > col
    eye = (row == col).astype(F32)
    tril3 = (lax.broadcasted_iota(jnp.int32, (C, 3 * C), 1) % C
             <= lax.broadcasted_iota(jnp.int32, (C, 3 * C), 0)).astype(BF16)

    bg = bg_ref[...]
    betas = [jnp.broadcast_to(bg[:, hh:hh + 1], (rows, LANES)) for hh in range(DN_HEADS)]

    chunks = [slice(c * C, (c + 1) * C) for c in range(nc)]
    units = [(hh, p) for hh in range(PREP_HEADS) for p in range(nc // 2)]
    cums = [_dot(tril3, jnp.concatenate(_split3(bg[rs, :]), axis=0)) for rs in chunks]
    gcs = [[jnp.broadcast_to(cum[:, DN_HEADS + hh:DN_HEADS + hh + 1], (C, LANES)) for cum in cums]
           for hh in range(DN_HEADS)]
    pick_t = (lax.broadcasted_iota(jnp.int32, (2 * SUBLANES, 3 * LANES), 1) % LANES
              == lax.broadcasted_iota(jnp.int32, (2 * SUBLANES, 3 * LANES), 0) + DN_HEADS).astype(BF16)
    cum_t = [_dot_nt(pick_t, jnp.concatenate([jnp.concatenate(_split3(cums[2 * p]), axis=1),
                                              jnp.concatenate(_split3(cums[2 * p + 1]), axis=1)], axis=0))
             for p in range(nc // 2)]
    rowforms = [jnp.broadcast_to(cum_t[p][hh:hh + 1, :], (C, pc)) for hh, p in units]
    decays, kbs, l_mats, qks = [], [], [], []
    for (hh, p), rowform in zip(units, rowforms):
        rs, hs = slice(p * pc, (p + 1) * pc), slice(hh * DN_DK, (hh + 1) * DN_DK)
        diff = jnp.where(is_a, gcs[hh][2 * p], gcs[hh][2 * p + 1]) - rowform
        decay = jnp.where(tril, jnp.exp(jnp.where(tril, diff, 0.0)), 0.0)
        k = k_ref[rs, hs]
        kb = k * betas[hh][rs, :]
        kbb, qb = kb.astype(BF16), q_ref[rs, hs].astype(BF16)
        res = _dot_nt(jnp.concatenate([kbb[:C], qb[:C], kbb[C:], qb[C:]], axis=0), k.astype(BF16))
        decays.append(decay)
        kbs.append(kb)
        l_mats.append(jnp.where(strict, jnp.where(is_a, res[:C], res[2 * C:3 * C]) * decay, 0.0))
        qks.append(jnp.where(is_a, res[C:2 * C], res[3 * C:]))

    invs = [eye] * len(units)
    s = 1
    while s < C:
        blk = ((row // (2 * s)) == (col // (2 * s))) & ((row % (2 * s)) >= s) & ((col % (2 * s)) < s)
        offs = [jnp.where(blk, l, 0.0) for l in l_mats]
        if s == 1:
            invs = [inv - off for inv, off in zip(invs, offs)]
        else:
            mids = [_pair_mm(off, inv) for off, inv in zip(offs, invs)]
            invs = [inv - _pair_mm(inv, mid) for inv, mid in zip(invs, mids)]
        s *= 2

    for n, (hh, p) in enumerate(units):
        rs, hs = slice(p * pc, (p + 1) * pc), slice(hh * DN_DK, (hh + 1) * DN_DK)
        ga, gb = gcs[hh][2 * p], gcs[hh][2 * p + 1]
        egc = jnp.exp(jnp.concatenate([ga, gb], axis=0))
        rhs = jnp.concatenate([(v_ref[rs, hs] * betas[hh][rs, :]).astype(BF16), (kbs[n] * egc).astype(BF16)], axis=1)
        uw = _dot(_pair_blockdiag(invs[n].astype(BF16)), rhs)
        u_ref[rs, hs] = uw[:, :DN_DV].astype(u_ref.dtype)
        w_ref[rs, hs] = uw[:, DN_DV:].astype(w_ref.dtype)
        intra_ref[hh, p * C:(p + 1) * C, :] = jnp.where(tril, qks[n] * decays[n], 0.0).astype(intra_ref.dtype)
        qd_ref[rs, hs] = (q_ref[rs, hs] * egc).astype(qd_ref.dtype)
        gla, glb = ga[C - 1:C, :], gb[C - 1:C, :]
        tail = jnp.exp(jnp.concatenate([gla - ga, glb - gb], axis=0))
        kt_ref[rs, hs] = (k_ref[rs, hs] * tail).astype(kt_ref.dtype)
        egl_ref[hh, 2 * p:2 * p + 1, :] = jnp.exp(gla)
        egl_ref[hh, 2 * p + 1:2 * p + 2, :] = jnp.exp(glb)


def _dn_prep(q, k, v, bg):
    t = q.shape[0]
    rows = PREP_ROWS
    blk = pl.BlockSpec((rows, PREP_HEADS * DN_DK), lambda i, h: (i, h))
    wide_bf = jax.ShapeDtypeStruct((t, DN_QK_W), BF16)
    return pl.pallas_call(
        _dn_prep_kernel,
        grid=(t // rows, DN_HEADS // PREP_HEADS),
        in_specs=[blk, blk, blk, pl.BlockSpec((rows, LANES), lambda i, h: (i, 0))],
        out_specs=[blk, blk, blk, blk, pl.BlockSpec((PREP_HEADS, rows // 2, 2 * C), lambda i, h: (h, i, 0)),
                   pl.BlockSpec((PREP_HEADS, rows // C, LANES), lambda i, h: (h, i, 0))],
        out_shape=[wide_bf, wide_bf, wide_bf, wide_bf,
                   jax.ShapeDtypeStruct((DN_HEADS, t // 2, 2 * C), BF16),
                   jax.ShapeDtypeStruct((DN_HEADS, t // C, LANES), F32)],
        compiler_params=_params(2),
        name="dn_prep",
    )(q, k, v, bg)


SCAN_ROWS = 512
SCAN_BATCH = 4


def _dn_scan_kernel(w_ref, qd_ref, kt_ref, u_ref, intra_ref, egl_ref, o_ref, state_ref):
    @pl.when(pl.program_id(1) == 0)
    def _():
        state_ref[...] = jnp.zeros_like(state_ref)

    nb, rows = w_ref.shape[0], w_ref.shape[1]
    chains = [(bb, h, slice(h * DN_DK, (h + 1) * DN_DK)) for bb in range(nb) for h in range(DN_HEADS)]
    states = [state_ref[bb, h] for bb, h, _ in chains]
    zeros = jnp.zeros((C, DN_DV), BF16)
    for c in range(rows // C):
        rs = slice(c * C, (c + 1) * C)
        r1s = [_dot(jnp.concatenate([w_ref[bb, rs, hs], qd_ref[bb, rs, hs]], axis=0), st.astype(BF16))
               for (bb, _, hs), st in zip(chains, states)]
        vbs = [(u_ref[bb, rs, hs] - r1[:C]).astype(BF16) for (bb, _, hs), r1 in zip(chains, r1s)]
        ps = slice((c // 2) * C, (c // 2 + 1) * C)
        vps = [jnp.concatenate([vb, zeros] if c % 2 == 0 else [zeros, vb], axis=0) for vb in vbs]
        outs = [r1[C:] + _dot(intra_ref[h, bb, ps, :], vp) for (bb, h, _), r1, vp in zip(chains, r1s, vps)]
        upds = [_dot_tn(kt_ref[bb, rs, hs], vb) for (bb, _, hs), vb in zip(chains, vbs)]
        for (bb, _, hs), out in zip(chains, outs):
            o_ref[bb, rs, hs] = out.astype(o_ref.dtype)
        states = [st * egl_ref[h, bb, c:c + 1, :] + upd for (bb, h, _), st, upd in zip(chains, states, upds)]
    for (bb, h, _), st in zip(chains, states):
        state_ref[bb, h] = st


def _dn_scan(w, qd, kt, u, intra, egl, batch, seq):
    rows = SCAN_ROWS
    nb = SCAN_BATCH
    as_bs = lambda a: a.reshape(batch, seq, a.shape[-1])
    wide = pl.BlockSpec((nb, rows, DN_QK_W), lambda b, i: (b, i, 0))
    intra4 = intra.reshape(DN_HEADS, batch, seq // 2, 2 * C)
    egl4 = egl.reshape(DN_HEADS, batch, seq // C, LANES)
    out = pl.pallas_call(
        _dn_scan_kernel,
        grid=(batch // nb, seq // rows),
        in_specs=[wide, wide, wide, wide,
                  pl.BlockSpec((DN_HEADS, nb, rows // 2, 2 * C), lambda b, i: (0, b, i, 0)),
                  pl.BlockSpec((DN_HEADS, nb, rows // C, LANES), lambda b, i: (0, b, i, 0))],
        out_specs=wide,
        out_shape=jax.ShapeDtypeStruct((batch, seq, DN_V_W), BF16),
        scratch_shapes=[pltpu.VMEM((nb, DN_HEADS, DN_DK, DN_DV), F32)],
        compiler_params=_params(2),
        name="dn_scan",
    )(as_bs(w), as_bs(qd), as_bs(kt), as_bs(u), intra4, egl4)
    return out.reshape(batch * seq, DN_V_W)


ATT_TQ = 512
ATT_TK = 512
NEG_INF = -1e30


def _mla_attn_kernel(qn_ref, qr_ref, kn_ref, kr_ref, v_ref, o_ref, qcat_ref, kcat_ref, vcat_ref):
    seq = qn_ref.shape[0]
    tq, tk = ATT_TQ, ATT_TK
    qcat_ref[:, :LANES] = qn_ref[...]
    qcat_ref[:, LANES:] = qr_ref[...]
    kcat_ref[:, :LANES] = kn_ref[...]
    kcat_ref[:, LANES:] = kr_ref[...]
    vcat_ref[:, :LANES] = v_ref[...]
    vcat_ref[:, LANES:] = jnp.ones((seq, LANES), BF16)
    hq = tq // 2
    causal = (lax.broadcasted_iota(jnp.int32, (hq, hq), 1) <= lax.broadcasted_iota(jnp.int32, (hq, hq), 0))
    masked = jnp.full((hq, hq), NEG_INF, F32)

    for i in range(seq // tq):
        q = qcat_ref[i * tq:(i + 1) * tq, :]
        m = acc = None
        for j in range(i + 1):
            k0 = j * tk
            if j < i:
                s = _dot_nt(q, kcat_ref[k0:k0 + tk, :])
            else:
                left = _dot_nt(q, kcat_ref[k0:k0 + hq, :])
                left = jnp.concatenate([jnp.where(causal, left[:hq], NEG_INF), left[hq:]], axis=0)
                right = jnp.where(causal, _dot_nt(q[hq:], kcat_ref[k0 + hq:k0 + tk, :]), NEG_INF)
                s = jnp.concatenate([left, jnp.concatenate([masked, right], axis=0)], axis=1)
            m_cur = jnp.max(s, axis=1, keepdims=True)
            if j == 0:
                m_new = jnp.broadcast_to(m_cur, (tq, LANES))
            else:
                m_new = jnp.maximum(m, m_cur)
            p = jnp.exp2(s - jnp.concatenate([m_new] * (tk // LANES), axis=1)).astype(BF16)
            if j < i:
                pv = _dot(p, vcat_ref[k0:k0 + tk, :])
            else:
                pv = _dot(p[:, :hq], vcat_ref[k0:k0 + hq, :])
                pv = pv + jnp.concatenate([jnp.zeros((hq, 2 * LANES), F32),
                                           _dot(p[hq:, hq:], vcat_ref[k0 + hq:k0 + tk, :])], axis=0)
            if j == 0:
                acc = pv
            else:
                alpha = jnp.exp2(m - m_new)
                acc = jnp.concatenate([alpha, alpha], axis=1) * acc + pv
            m = m_new
        o_ref[i * tq:(i + 1) * tq, :] = (acc[:, :LANES] / acc[:, LANES:]).astype(o_ref.dtype)


def _mla_attn(qn, qr, kn, kr, v, batch, seq):
    t = qn.shape[0]
    blk = pl.BlockSpec((seq, LANES), lambda b, h: (b, h))
    return pl.pallas_call(
        _mla_attn_kernel,
        grid=(batch, MLA_HEADS),
        in_specs=[blk, blk, blk, pl.BlockSpec((seq, LANES), lambda b, h: (b, 0)), blk],
        out_specs=blk,
        out_shape=jax.ShapeDtypeStruct((t, MLA_HEADS * MLA_V), BF16),
        scratch_shapes=[pltpu.VMEM((seq, 2 * LANES), BF16)] * 3,
        compiler_params=_params(2),
        name="mla_attn",
    )(qn, qr, kn, kr, v)


MERGE_TM = 512
MERGE_ROWS = 256


def _merge_kernel(x_ref, odn_ref, omla_ref, wzg_ref, wdn_ref, wmla_ref, wo_ref, nw_ref, g_ref, b_ref, h_ref):
    tm = x_ref.shape[0]
    groups = [slice(r * MERGE_ROWS, (r + 1) * MERGE_ROWS) for r in range(tm // MERGE_ROWS)]
    xs = [x_ref[rs, :] for rs in groups]
    zgs = [_dot(x.astype(BF16), wzg_ref[...]) for x in xs]
    y_mlas = [_dot(omla_ref[rs, :], wmla_ref[...]) for rs in groups]
    o_dns = []
    for rs, zg in zip(groups, zgs):
        parts = [_rms(odn_ref[rs, h * DN_DV:(h + 1) * DN_DV].astype(F32), nw_ref[...]) for h in range(DN_HEADS)]
        o_dns.append((jnp.concatenate(parts, axis=1) * _silu(zg[:, :DN_V_W])).astype(BF16))
    y_dns = [_dot(o_dn, wdn_ref[...]) for o_dn in o_dns]
    mixeds = [(_sigmoid(zg[:, DN_V_W:DN_V_W + D_MODEL]) * y_dn + _sigmoid(zg[:, DN_V_W + D_MODEL:]) * y_mla).astype(BF16)
              for zg, y_dn, y_mla in zip(zgs, y_dns, y_mlas)]
    outs = [_dot(mixed, wo_ref[...]) for mixed in mixeds]
    for rs, x, out in zip(groups, xs, outs):
        h_ref[rs, :] = _layer_norm(DEEPNORM_ALPHA * x + out, g_ref[...], b_ref[...])


def _merge(x2, o_dn, o_mla, wzg, wdn, wmla, wo, nw, g, b):
    t = x2.shape[0]
    tm = MERGE_TM
    row = lambda i: (i, 0)
    act = pl.BlockSpec((tm, D_MODEL), row)
    return pl.pallas_call(
        _merge_kernel,
        grid=(t // tm,),
        in_specs=[act, act, act, _const_spec(wzg.shape), _const_spec(wdn.shape), _const_spec(wmla.shape),
                  _const_spec(wo.shape), _const_spec(nw.shape), _const_spec(g.shape), _const_spec(b.shape)],
        out_specs=act,
        out_shape=jax.ShapeDtypeStruct((t, D_MODEL), F32),
        compiler_params=_params(1),
        name="merge",
    )(x2, o_dn, o_mla, wzg, wdn, wmla, wo, nw, g, b)


FFN_TM = 1024
FFN_SLAB = 256


def _ffn_kernel(h_ref, p_ref, win_ref, wout_ref, wpg_ref, wple_ref, g_ref, b_ref, o_ref):
    h = h_ref[...]
    hb = h.astype(BF16)
    acc = jnp.zeros(h.shape, F32)
    for j in range(FFN_HIDDEN // FFN_SLAB):
        cs = slice(j * FFN_SLAB, (j + 1) * FFN_SLAB)
        gt = _dot(hb, win_ref[:, cs])
        up = _dot(hb, win_ref[:, FFN_HIDDEN + j * FFN_SLAB:FFN_HIDDEN + (j + 1) * FFN_SLAB])
        a = (_silu(gt) * up).astype(BF16)
        acc = acc + _dot(a, wout_ref[cs, :])
    ple = _sigmoid(_dot(hb, wpg_ref[...])) * _dot(p_ref[...].astype(BF16), wple_ref[...])
    r = DEEPNORM_ALPHA * h + acc + ple
    o_ref[...] = _layer_norm(r, g_ref[...], b_ref[...])


def _ffn(h1, p2, win, wout, wpg, wple, g, b):
    assert FFN_HIDDEN % FFN_SLAB == 0
    t = h1.shape[0]
    tm = FFN_TM
    row = lambda i: (i, 0)
    act = pl.BlockSpec((tm, D_MODEL), row)
    return pl.pallas_call(
        _ffn_kernel,
        grid=(t // tm,),
        in_specs=[act, pl.BlockSpec((tm, PLE_DIM), row), _const_spec(win.shape), _const_spec(wout.shape),
                  _const_spec(wpg.shape), _const_spec(wple.shape), _const_spec(g.shape), _const_spec(b.shape)],
        out_specs=act,
        out_shape=jax.ShapeDtypeStruct((t, D_MODEL), F32),
        compiler_params=_params(1),
        name="ffn",
    )(h1, p2, win, wout, wpg, wple, g, b)


def _split_w_in(w_in):
    offs = [0]
    for n in IN_SIZES:
        offs.append(offs[-1] + n)
    return [w_in[:, offs[i]:offs[i + 1]] for i in range(len(IN_SIZES))]


def kernel(x, p, positions, w_in, conv_w, dn_a_log, dn_dt_bias, dn_norm_w, q_norm_w, w_uq, kv_norm_w, w_uk, w_uv,
           w_br_dn, w_br_mla, w_o, ln1_g, ln1_b, w_ffn_in, w_ffn_out, w_ple, w_ple_gate, ln2_g, ln2_b):
    assert w_in.shape[0] == DEPTH
    batch, seq, _ = x.shape
    t = batch * seq
    assert x.shape[2] == D_MODEL and batch % SCAN_BATCH == 0
    assert all(seq % n == 0 for n in (QKV_TM, PREP_ROWS, SCAN_ROWS, ATT_TQ, ATT_TK)) and ATT_TQ == ATT_TK
    assert t % MERGE_TM == 0 and t % FFN_TM == 0 and MERGE_TM % MERGE_ROWS == 0
    h = x.reshape(t, D_MODEL)
    pos2 = positions.reshape(t, 1)
    half = MLA_ROPE // 2
    invf = ROPE_BASE ** (-jnp.arange(0, MLA_ROPE, 2, dtype=F32) / MLA_ROPE)
    invf = jnp.tile(invf, LANES // half).reshape(1, LANES)

    for i in range(DEPTH):
        w_in_bf = w_in[i].astype(BF16)
        _, w_z, w_b, w_a, w_cq, w_ckv, w_kr, w_gdn, w_gmla = _split_w_in(w_in_bf)
        w_kr_sw = jnp.concatenate([w_kr[:, half:], w_kr[:, :half]], axis=1)
        pad = jnp.zeros((D_MODEL, LAT_W - LAT_BA - 2 * DN_HEADS), BF16)
        w_lat = jnp.concatenate([w_cq, w_ckv, w_kr, w_kr_sw, w_b, w_a, pad], axis=1)
        w_zg = jnp.concatenate([w_z, w_gdn, w_gmla], axis=1)
        uq = w_uq[i]
        t1 = uq[:, :, MLA_NOPE:MLA_NOPE + half]
        t2 = uq[:, :, MLA_NOPE + half:]
        wqn = uq[:, :, :MLA_NOPE].reshape(MLA_Q_LORA, MLA_HEADS * MLA_NOPE).astype(BF16)
        wqr = jnp.concatenate([t1, t2, t2, t1], axis=2).reshape(MLA_Q_LORA, MLA_HEADS * LANES).astype(BF16)
        wuk = w_uk[i].reshape(MLA_KV_LORA, MLA_HEADS * MLA_NOPE).astype(BF16)
        wuv = w_uv[i].reshape(MLA_KV_LORA, MLA_HEADS * MLA_V).astype(BF16)
        lane_pad = jnp.zeros((LANES - 2 * DN_HEADS,), F32)
        head_pad = jnp.zeros((DN_HEADS,), F32)
        dnp = jnp.stack([jnp.concatenate([head_pad, dn_a_log[i], lane_pad]),
                         jnp.concatenate([head_pad, dn_dt_bias[i], lane_pad])])
        nw = jnp.tile(dn_norm_w[i], 1).reshape(1, DN_DV)
        w_ffn = w_ffn_in[i].astype(BF16)

        q, k, v, qn, qr, kn, kr, vm, bg = _input_stage(
            h, w_in_bf, conv_w[i], w_lat, pos2, invf, q_norm_w[i].reshape(1, -1), kv_norm_w[i].reshape(1, -1),
            wqn, wqr, wuk, wuv, dnp, seq)
        w, qd, kt, u, intra, egl = _dn_prep(q, k, v, bg)
        o_dn = _dn_scan(w, qd, kt, u, intra, egl, batch, seq)
        o_mla = _mla_attn(qn, qr, kn, kr, vm, batch, seq)
        h1 = _merge(h, o_dn, o_mla, w_zg, w_br_dn[i].astype(BF16), w_br_mla[i].astype(BF16),
                    w_o[i].astype(BF16), nw, ln1_g[i].reshape(1, -1), ln1_b[i].reshape(1, -1))
        h = _ffn(h1, p[i].reshape(t, PLE_DIM), w_ffn,
                 w_ffn_out[i].astype(BF16), w_ple_gate[i].astype(BF16), w_ple[i].astype(BF16),
                 ln2_g[i].reshape(1, -1), ln2_b[i].reshape(1, -1))
    return h.reshape(batch, seq, D_MODEL)
```

```python
import functools
import math

import jax
import jax.numpy as jnp
from jax import lax
from jax.experimental import pallas as pl
from jax.experimental.pallas import tpu as pltpu

D_MODEL = 1024
PLE_DIM = 256
DN_HEADS = 8
DN_DK = 128
DN_DV = 128
DN_CONV = 4
DN_CHUNK = 64
DN_QK_W = DN_HEADS * DN_DK
DN_V_W = DN_HEADS * DN_DV
DN_CONV_CH = 2 * DN_QK_W + DN_V_W
MLA_HEADS = 8
MLA_Q_LORA = 384
MLA_KV_LORA = 256
MLA_NOPE = 128
MLA_ROPE = 64
MLA_V = 128
ROPE_BASE = 10000.0
FFN_HIDDEN = -(-8 * D_MODEL // (3 * 256)) * 256
DEPTH = 1
DEEPNORM_ALPHA = (2.0 * DEPTH) ** 0.25
IN_SIZES = (DN_CONV_CH, DN_V_W, DN_HEADS, DN_HEADS, MLA_Q_LORA, MLA_KV_LORA, MLA_ROPE, D_MODEL, D_MODEL)

LANES = 128
SUBLANES = 8
VMEM_LIMIT_BYTES = 56 * 1024 * 1024

LAT_W = 896
LAT_CKV = MLA_Q_LORA
LAT_KR = LAT_CKV + MLA_KV_LORA
LAT_BA = LAT_KR + 2 * MLA_ROPE

BF16 = jnp.bfloat16
F32 = jnp.float32


def _dot(a, b):
    return jnp.dot(a, b, preferred_element_type=F32)


def _dot_nt(a, b):
    return lax.dot_general(a, b, (((1,), (1,)), ((), ())), preferred_element_type=F32)


def _dot_tn(a, b):
    return lax.dot_general(a, b, (((0,), (0,)), ((), ())), preferred_element_type=F32)


def _sigmoid(x):
    return 0.5 * jnp.tanh(0.5 * x) + 0.5


def _silu(x):
    h = 0.5 * x
    return h * jnp.tanh(h) + h


def _layer_norm(r, g, b):
    mu = jnp.mean(r, axis=-1, keepdims=True)
    c = r - mu
    var = jnp.mean(c * c, axis=-1, keepdims=True)
    return c * lax.rsqrt(var + 1e-5) * g + b


def _const_spec(shape):
    return pl.BlockSpec(shape, lambda *_: (0,) * len(shape))


def _params(n_axes, sem=None):
    return pltpu.CompilerParams(
        dimension_semantics=sem or ("arbitrary",) * n_axes, vmem_limit_bytes=VMEM_LIMIT_BYTES)


QKV_TM = 512
QKV_SLAB = 512
QKV_ROWS = 64


def _dn_qkv_kernel(tiles_per_seq, x_ref, w_ref, cw_ref, q_ref, k_ref, v_ref, pre_ref, tail_ref):
    i = pl.program_id(0)
    tm = x_ref.shape[0]
    assert DN_CONV == 4

    @pl.when(i % tiles_per_seq == 0)
    def _():
        pre_ref[:SUBLANES, :] = jnp.zeros((SUBLANES, DN_CONV_CH), F32)
        tail_ref[...] = jnp.zeros_like(tail_ref)

    xb = x_ref[...].astype(BF16)
    outs = (q_ref, k_ref, v_ref)
    slabs_per_out = DN_QK_W // QKV_SLAB
    for j in range(DN_CONV_CH // QKV_SLAB):
        cs = slice(j * QKV_SLAB, (j + 1) * QKV_SLAB)
        pre_ref[SUBLANES:, cs] = _dot(xb, w_ref[:, cs])
        cw = 0.5 * cw_ref[:, cs]
        out_ref = outs[j // slabs_per_out]
        col0 = (j % slabs_per_out) * QKV_SLAB
        which = j // slabs_per_out
        pair_tail = tail_ref[:, cs]
        for r in range(tm // QKV_ROWS):
            ext = pre_ref[r * QKV_ROWS:(r + 1) * QKV_ROWS + SUBLANES, cs]
            pre = ext[SUBLANES:]
            prev = pltpu.roll(ext, 1, axis=0)[SUBLANES:]
            pair = pre * cw[1:2] + prev * cw[0:1]
            pair2 = pltpu.roll(jnp.concatenate([pair_tail, pair], axis=0), 2, axis=0)[SUBLANES:]
            pair_tail = pair[QKV_ROWS - SUBLANES:]
            half_y = pre * cw[3:4] + prev * cw[2:3] + pair2
            y = half_y * jnp.tanh(half_y) + half_y
            rs = slice(r * QKV_ROWS, (r + 1) * QKV_ROWS)
            for hh in range(QKV_SLAB // DN_DK):
                yh = y[:, hh * DN_DK:(hh + 1) * DN_DK]
                if which < 2:
                    ss = jnp.sum(yh * yh, axis=-1, keepdims=True)
                    scale = lax.rsqrt(ss + 1e-6)
                    yh = yh * (scale * (DN_DK ** -0.5) if which == 0 else scale)
                out_ref[rs, col0 + hh * DN_DK:col0 + (hh + 1) * DN_DK] = yh.astype(out_ref.dtype)
        tail_ref[:, cs] = pair_tail
        pre_ref[:SUBLANES, cs] = pre_ref[tm:, cs]


MLA_SCALE = (MLA_NOPE + MLA_ROPE) ** -0.5 * math.log2(math.e)


def _rms(t, w):
    return t * lax.rsqrt(jnp.mean(t * t, axis=-1, keepdims=True) + 1e-6) * w


def _latent_kernel(x_ref, wl_ref, pos_ref, invf_ref, qnw_ref, kvw_ref, wqn_ref, wqr_ref, wuk_ref, wuv_ref,
                   dnp_ref, qn_ref, qr_ref, kn_ref, kr_ref, v_ref, bg_ref):
    xb = x_ref[...].astype(BF16)
    lat = _dot(xb, wl_ref[...])

    tm = x_ref.shape[0]
    qt = tm // 4
    half = MLA_ROPE // 2
    lane = lax.broadcasted_iota(jnp.int32, (1, LANES), 1)
    pos = pos_ref[...].astype(F32)
    grp = lane // half
    pos4 = jnp.where(grp == 0, pos[:qt], jnp.where(grp == 1, pos[qt:2 * qt],
                                                   jnp.where(grp == 2, pos[2 * qt:3 * qt], pos[3 * qt:])))
    ang = pos4 * invf_ref[...]
    cos3 = jnp.concatenate(_split3(jnp.cos(ang)), axis=1)
    sin3 = jnp.concatenate(_split3(jnp.sin(ang)), axis=1)
    src = lax.broadcasted_iota(jnp.int32, (3 * LANES, LANES), 0) % LANES
    dst = lax.broadcasted_iota(jnp.int32, (3 * LANES, LANES), 1)
    cos_parts, sin_parts = [], []
    for g in range(4):
        hit = (src // half == g) & (src % half == dst % half) & (dst < MLA_ROPE)
        cos_parts.append(_dot(cos3, jnp.where(hit, 1.0, 0.0).astype(BF16)))
        sin_parts.append(_dot(sin3, jnp.where(hit, jnp.where(dst < half, -1.0, 1.0), 0.0).astype(BF16)))
    cos_t = jnp.concatenate(cos_parts, axis=0)
    sin_t = jnp.concatenate(sin_parts, axis=0)

    def rope(a):
        return a * cos_t + pltpu.roll(a, MLA_ROPE, axis=1) * sin_t

    c_q = _rms(lat[:, :MLA_Q_LORA], qnw_ref[...] * MLA_SCALE).astype(BF16)
    qn_ref[...] = _dot(c_q, wqn_ref[...]).astype(qn_ref.dtype)
    qr_raw = _dot(c_q, wqr_ref[...])
    for h in range(MLA_HEADS):
        hs = slice(h * LANES, (h + 1) * LANES)
        qr_ref[:, hs] = rope(qr_raw[:, hs]).astype(qr_ref.dtype)

    c_kv = _rms(lat[:, LAT_CKV:LAT_KR], kvw_ref[...]).astype(BF16)
    kn_ref[...] = _dot(c_kv, wuk_ref[...]).astype(kn_ref.dtype)
    v_ref[...] = _dot(c_kv, wuv_ref[...]).astype(v_ref.dtype)
    kr_ref[...] = rope(lat[:, LAT_KR:LAT_BA]).astype(kr_ref.dtype)

    ba = lat[:, LAT_BA:LAT_BA + LANES]
    beta = _sigmoid(ba)
    sp_in = ba + dnp_ref[1:2, :]
    softplus = jnp.maximum(sp_in, 0.0) + jnp.log(1.0 + jnp.exp(-jnp.abs(sp_in)))
    g = -jnp.exp(dnp_ref[0:1, :]) * softplus
    bg_ref[...] = jnp.where(lane < DN_HEADS, beta, jnp.where(lane < 2 * DN_HEADS, g, 0.0))


def _input_kernel(tiles_per_seq, x_ref, wqkv_ref, cw_ref, wl_ref, pos_ref, invf_ref, qnw_ref, kvw_ref, wqn_ref,
                  wqr_ref, wuk_ref, wuv_ref, dnp_ref, q_ref, k_ref, v_ref, qn_ref, qr_ref, kn_ref, kr_ref, vm_ref,
                  bg_ref, pre_ref, tail_ref):
    _dn_qkv_kernel(tiles_per_seq, x_ref, wqkv_ref, cw_ref, q_ref, k_ref, v_ref, pre_ref, tail_ref)
    _latent_kernel(x_ref, wl_ref, pos_ref, invf_ref, qnw_ref, kvw_ref, wqn_ref, wqr_ref, wuk_ref, wuv_ref, dnp_ref,
                   qn_ref, qr_ref, kn_ref, kr_ref, vm_ref, bg_ref)


def _input_stage(x2, w_in_bf, conv_w, w_lat, pos2, invf, qnw, kvw, wqn, wqr, wuk, wuv, dnp, seq):
    t = x2.shape[0]
    tm = QKV_TM
    row = lambda i: (i, 0)
    dn_out = jax.ShapeDtypeStruct((t, DN_QK_W), BF16)
    wide = jax.ShapeDtypeStruct((t, MLA_HEADS * LANES), BF16)
    wide_spec = pl.BlockSpec((tm, MLA_HEADS * LANES), row)
    lane_spec = pl.BlockSpec((tm, LANES), row)
    consts = [conv_w, w_lat]
    consts2 = [invf, qnw, kvw, wqn, wqr, wuk, wuv, dnp]
    return pl.pallas_call(
        functools.partial(_input_kernel, seq // tm),
        grid=(t // tm,),
        in_specs=[pl.BlockSpec((tm, D_MODEL), row), _const_spec((D_MODEL, DN_CONV_CH))]
        + [_const_spec(a.shape) for a in consts] + [pl.BlockSpec((tm, 1), row)]
        + [_const_spec(a.shape) for a in consts2],
        out_specs=[pl.BlockSpec((tm, DN_QK_W), row)] * 3 + [wide_spec] * 3 + [lane_spec, wide_spec, lane_spec],
        out_shape=[dn_out] * 3 + [wide, wide, wide, jax.ShapeDtypeStruct((t, LANES), BF16), wide,
                                  jax.ShapeDtypeStruct((t, LANES), F32)],
        scratch_shapes=[pltpu.VMEM((SUBLANES + tm, DN_CONV_CH), F32), pltpu.VMEM((SUBLANES, DN_CONV_CH), F32)],
        compiler_params=_params(1),
        name="input_stage",
    )(x2, w_in_bf, conv_w, w_lat, pos2, invf, qnw, kvw, wqn, wqr, wuk, wuv, dnp)


PREP_ROWS = 1024
PREP_HEADS = 8
C = DN_CHUNK


def _split3(a):
    a1 = a.astype(BF16)
    r1 = a - a1.astype(F32)
    a2 = r1.astype(BF16)
    a3 = (r1 - a2.astype(F32)).astype(BF16)
    return a1, a2, a3


def _pair_blockdiag(m):
    lane = lax.broadcasted_iota(jnp.int32, m.shape, 1)
    zero = jnp.zeros_like(m)
    return jnp.concatenate([jnp.where(lane < C, m, zero), jnp.where(lane >= C, m, zero)], axis=0)


def _pair_mm(a, b):
    return _dot(a.astype(BF16), _pair_blockdiag(b.astype(BF16)))


def _dn_prep_kernel(q_ref, k_ref, v_ref, bg_ref, w_ref, qd_ref, kt_ref, u_ref, intra_ref, egl_ref):
    assert PREP_HEADS == DN_HEADS
    rows = q_ref.shape[0]
    nc = rows // C
    pc = 2 * C

    row = lax.broadcasted_iota(jnp.int32, (C, pc), 0)
    lane = lax.broadcasted_iota(jnp.int32, (C, pc), 1)
    col = lane % C
    is_a = lane < C
    tril = row >= col
    strict = row > col
    eye = (row == col).astype(F32)
    tril3 = (lax.broadcasted_iota(jnp.int32, (C, 3 * C), 1) % C
             <= lax.broadcasted_iota(jnp.int32, (C, 3 * C), 0)).astype(BF16)

    bg = bg_ref[...]
    betas = [jnp.broadcast_to(bg[:, hh:hh + 1], (rows, LANES)) for hh in range(DN_HEADS)]

    chunks = [slice(c * C, (c + 1) * C) for c in range(nc)]
    units = [(hh, p) for hh in range(PREP_HEADS) for p in range(nc // 2)]
    cums = [_dot(tril3, jnp.concatenate(_split3(bg[rs, :]), axis=0)) for rs in chunks]
    gcs = [[jnp.broadcast_to(cum[:, DN_HEADS + hh:DN_HEADS + hh + 1], (C, LANES)) for cum in cums]
           for hh in range(DN_HEADS)]
    pick_t = (lax.broadcasted_iota(jnp.int32, (2 * SUBLANES, 3 * LANES), 1) % LANES
              == lax.broadcasted_iota(jnp.int32, (2 * SUBLANES, 3 * LANES), 0) + DN_HEADS).astype(BF16)
    cum_t = [_dot_nt(pick_t, jnp.concatenate([jnp.concatenate(_split3(cums[2 * p]), axis=1),
                                              jnp.concatenate(_split3(cums[2 * p + 1]), axis=1)], axis=0))
             for p in range(nc // 2)]
    rowforms = [jnp.broadcast_to(cum_t[p][hh:hh + 1, :], (C, pc)) for hh, p in units]
    decays, kbs, l_mats, qks = [], [], [], []
    for (hh, p), rowform in zip(units, rowforms):
        rs, hs = slice(p * pc, (p + 1) * pc), slice(hh * DN_DK, (hh + 1) * DN_DK)
        diff = jnp.where(is_a, gcs[hh][2 * p], gcs[hh][2 * p + 1]) - rowform
        decay = jnp.where(tril, jnp.exp(jnp.where(tril, diff, 0.0)), 0.0)
        k = k_ref[rs, hs]
        kb = k * betas[hh][rs, :]
        kbb, qb = kb.astype(BF16), q_ref[rs, hs].astype(BF16)
        res = _dot_nt(jnp.concatenate([kbb[:C], qb[:C], kbb[C:], qb[C:]], axis=0), k.astype(BF16))
        decays.append(decay)
        kbs.append(kb)
        l_mats.append(jnp.where(strict, jnp.where(is_a, res[:C], res[2 * C:3 * C]) * decay, 0.0))
        qks.append(jnp.where(is_a, res[C:2 * C], res[3 * C:]))

    invs = [eye] * len(units)
    s = 1
    while s < C:
        blk = ((row // (2 * s)) == (col // (2 * s))) & ((row % (2 * s)) >= s) & ((col % (2 * s)) < s)
        offs = [jnp.where(blk, l, 0.0) for l in l_mats]
        if s == 1:
            invs = [inv - off for inv, off in zip(invs, offs)]
        else:
            mids = [_pair_mm(off, inv) for off, inv in zip(offs, invs)]
            invs = [inv - _pair_mm(inv, mid) for inv, mid in zip(invs, mids)]
        s *= 2

    for n, (hh, p) in enumerate(units):
        rs, hs = slice(p * pc, (p + 1) * pc), slice(hh * DN_DK, (hh + 1) * DN_DK)
        ga, gb = gcs[hh][2 * p], gcs[hh][2 * p + 1]
        egc = jnp.exp(jnp.concatenate([ga, gb], axis=0))
        rhs = jnp.concatenate([(v_ref[rs, hs] * betas[hh][rs, :]).astype(BF16), (kbs[n] * egc).astype(BF16)], axis=1)
        uw = _dot(_pair_blockdiag(invs[n].astype(BF16)), rhs)
        u_ref[rs, hs] = uw[:, :DN_DV].astype(u_ref.dtype)
        w_ref[rs, hs] = uw[:, DN_DV:].astype(w_ref.dtype)
        intra_ref[hh, p * C:(p + 1) * C, :] = jnp.where(tril, qks[n] * decays[n], 0.0).astype(intra_ref.dtype)
        qd_ref[rs, hs] = (q_ref[rs, hs] * egc).astype(qd_ref.dtype)
        gla, glb = ga[C - 1:C, :], gb[C - 1:C, :]
        tail = jnp.exp(jnp.concatenate([gla - ga, glb - gb], axis=0))
        kt_ref[rs, hs] = (k_ref[rs, hs] * tail).astype(kt_ref.dtype)
        egl_ref[hh, 2 * p:2 * p + 1, :] = jnp.exp(gla)
        egl_ref[hh, 2 * p + 1:2 * p + 2, :] = jnp.exp(glb)


def _dn_prep(q, k, v, bg):
    t = q.shape[0]
    rows = PREP_ROWS
    blk = pl.BlockSpec((rows, PREP_HEADS * DN_DK), lambda i, h: (i, h))
    wide_bf = jax.ShapeDtypeStruct((t, DN_QK_W), BF16)
    return pl.pallas_call(
        _dn_prep_kernel,
        grid=(t // rows, DN_HEADS // PREP_HEADS),
        in_specs=[blk, blk, blk, pl.BlockSpec((rows, LANES), lambda i, h: (i, 0))],
        out_specs=[blk, blk, blk, blk, pl.BlockSpec((PREP_HEADS, rows // 2, 2 * C), lambda i, h: (h, i, 0)),
                   pl.BlockSpec((PREP_HEADS, rows // C, LANES), lambda i, h: (h, i, 0))],
        out_shape=[wide_bf, wide_bf, wide_bf, wide_bf,
                   jax.ShapeDtypeStruct((DN_HEADS, t // 2, 2 * C), BF16),
                   jax.ShapeDtypeStruct((DN_HEADS, t // C, LANES), F32)],
        compiler_params=_params(2),
        name="dn_prep",
    )(q, k, v, bg)


SCAN_ROWS = 512
SCAN_BATCH = 4


def _dn_scan_kernel(w_ref, qd_ref, kt_ref, u_ref, intra_ref, egl_ref, o_ref, state_ref):
    @pl.when(pl.program_id(1) == 0)
    def _():
        state_ref[...] = jnp.zeros_like(state_ref)

    nb, rows = w_ref.shape[0], w_ref.shape[1]
    chains = [(bb, h, slice(h * DN_DK, (h + 1) * DN_DK)) for bb in range(nb) for h in range(DN_HEADS)]
    states = [state_ref[bb, h] for bb, h, _ in chains]
    zeros = jnp.zeros((C, DN_DV), BF16)
    for c in range(rows // C):
        rs = slice(c * C, (c + 1) * C)
        r1s = [_dot(jnp.concatenate([w_ref[bb, rs, hs], qd_ref[bb, rs, hs]], axis=0), st.astype(BF16))
               for (bb, _, hs), st in zip(chains, states)]
        vbs = [(u_ref[bb, rs, hs] - r1[:C]).astype(BF16) for (bb, _, hs), r1 in zip(chains, r1s)]
        ps = slice((c // 2) * C, (c // 2 + 1) * C)
        vps = [jnp.concatenate([vb, zeros] if c % 2 == 0 else [zeros, vb], axis=0) for vb in vbs]
        outs = [r1[C:] + _dot(intra_ref[h, bb, ps, :], vp) for (bb, h, _), r1, vp in zip(chains, r1s, vps)]
        upds = [_dot_tn(kt_ref[bb, rs, hs], vb) for (bb, _, hs), vb in zip(chains, vbs)]
        for (bb, _, hs), out in zip(chains, outs):
            o_ref[bb, rs, hs] = out.astype(o_ref.dtype)
        states = [st * egl_ref[h, bb, c:c + 1, :] + upd for (bb, h, _), st, upd in zip(chains, states, upds)]
    for (bb, h, _), st in zip(chains, states):
        state_ref[bb, h] = st


def _dn_scan(w, qd, kt, u, intra, egl, batch, seq):
    rows = SCAN_ROWS
    nb = SCAN_BATCH
    as_bs = lambda a: a.reshape(batch, seq, a.shape[-1])
    wide = pl.BlockSpec((nb, rows, DN_QK_W), lambda b, i: (b, i, 0))
    intra4 = intra.reshape(DN_HEADS, batch, seq // 2, 2 * C)
    egl4 = egl.reshape(DN_HEADS, batch, seq // C, LANES)
    out = pl.pallas_call(
        _dn_scan_kernel,
        grid=(batch // nb, seq // rows),
        in_specs=[wide, wide, wide, wide,
                  pl.BlockSpec((DN_HEADS, nb, rows // 2, 2 * C), lambda b, i: (0, b, i, 0)),
                  pl.BlockSpec((DN_HEADS, nb, rows // C, LANES), lambda b, i: (0, b, i, 0))],
        out_specs=wide,
        out_shape=jax.ShapeDtypeStruct((batch, seq, DN_V_W), BF16),
        scratch_shapes=[pltpu.VMEM((nb, DN_HEADS, DN_DK, DN_DV), F32)],
        compiler_params=_params(2),
        name="dn_scan",
    )(as_bs(w), as_bs(qd), as_bs(kt), as_bs(u), intra4, egl4)
    return out.reshape(batch * seq, DN_V_W)


ATT_TQ = 512
ATT_TK = 512
NEG_INF = -1e30


def _mla_attn_kernel(qn_ref, qr_ref, kn_ref, kr_ref, v_ref, o_ref, qcat_ref, kcat_ref, vcat_ref):
    seq = qn_ref.shape[0]
    tq, tk = ATT_TQ, ATT_TK
    qcat_ref[:, :LANES] = qn_ref[...]
    qcat_ref[:, LANES:] = qr_ref[...]
    kcat_ref[:, :LANES] = kn_ref[...]
    kcat_ref[:, LANES:] = kr_ref[...]
    vcat_ref[:, :LANES] = v_ref[...]
    vcat_ref[:, LANES:] = jnp.ones((seq, LANES), BF16)
    hq = tq // 2
    causal = (lax.broadcasted_iota(jnp.int32, (hq, hq), 1) <= lax.broadcasted_iota(jnp.int32, (hq, hq), 0))
    masked = jnp.full((hq, hq), NEG_INF, F32)

    for i in range(seq // tq):
        q = qcat_ref[i * tq:(i + 1) * tq, :]
        m = acc = None
        for j in range(i + 1):
            k0 = j * tk
            if j < i:
                s = _dot_nt(q, kcat_ref[k0:k0 + tk, :])
            else:
                left = _dot_nt(q, kcat_ref[k0:k0 + hq, :])
                left = jnp.concatenate([jnp.where(causal, left[:hq], NEG_INF), left[hq:]], axis=0)
                right = jnp.where(causal, _dot_nt(q[hq:], kcat_ref[k0 + hq:k0 + tk, :]), NEG_INF)
                s = jnp.concatenate([left, jnp.concatenate([masked, right], axis=0)], axis=1)
            m_cur = jnp.max(s, axis=1, keepdims=True)
            if j == 0:
                m_new = jnp.broadcast_to(m_cur, (tq, LANES))
            else:
                m_new = jnp.maximum(m, m_cur)
            p = jnp.exp2(s - jnp.concatenate([m_new] * (tk // LANES), axis=1)).astype(BF16)
            if j < i:
                pv = _dot(p, vcat_ref[k0:k0 + tk, :])
            else:
                pv = _dot(p[:, :hq], vcat_ref[k0:k0 + hq, :])
                pv = pv + jnp.concatenate([jnp.zeros((hq, 2 * LANES), F32),
                                           _dot(p[hq:, hq:], vcat_ref[k0 + hq:k0 + tk, :])], axis=0)
            if j == 0:
                acc = pv
            else:
                alpha = jnp.exp2(m - m_new)
                acc = jnp.concatenate([alpha, alpha], axis=1) * acc + pv
            m = m_new
        o_ref[i * tq:(i + 1) * tq, :] = (acc[:, :LANES] / acc[:, LANES:]).astype(o_ref.dtype)


def _mla_attn(qn, qr, kn, kr, v, batch, seq):
    t = qn.shape[0]
    blk = pl.BlockSpec((seq, LANES), lambda b, h: (b, h))
    return pl.pallas_call(
        _mla_attn_kernel,
        grid=(batch, MLA_HEADS),
        in_specs=[blk, blk, blk, pl.BlockSpec((seq, LANES), lambda b, h: (b, 0)), blk],
        out_specs=blk,
        out_shape=jax.ShapeDtypeStruct((t, MLA_HEADS * MLA_V), BF16),
        scratch_shapes=[pltpu.VMEM((seq, 2 * LANES), BF16)] * 3,
        compiler_params=_params(2),
        name="mla_attn",
    )(qn, qr, kn, kr, v)


MERGE_TM = 1024
MERGE_ROWS = 256


def _merge_kernel(x_ref, odn_ref, omla_ref, wzg_ref, wdn_ref, wmla_ref, wo_ref, nw_ref, g_ref, b_ref, h_ref):
    tm = x_ref.shape[0]
    groups = [slice(r * MERGE_ROWS, (r + 1) * MERGE_ROWS) for r in range(tm // MERGE_ROWS)]
    xs = [x_ref[rs, :] for rs in groups]
    zgs = [_dot(x.astype(BF16), wzg_ref[...]) for x in xs]
    y_mlas = [_dot(omla_ref[rs, :], wmla_ref[...]) for rs in groups]
    o_dns = []
    for rs, zg in zip(groups, zgs):
        parts = [_rms(odn_ref[rs, h * DN_DV:(h + 1) * DN_DV].astype(F32), nw_ref[...]) for h in range(DN_HEADS)]
        o_dns.append((jnp.concatenate(parts, axis=1) * _silu(zg[:, :DN_V_W])).astype(BF16))
    y_dns = [_dot(o_dn, wdn_ref[...]) for o_dn in o_dns]
    mixeds = [(_sigmoid(zg[:, DN_V_W:DN_V_W + D_MODEL]) * y_dn + _sigmoid(zg[:, DN_V_W + D_MODEL:]) * y_mla).astype(BF16)
              for zg, y_dn, y_mla in zip(zgs, y_dns, y_mlas)]
    outs = [_dot(mixed, wo_ref[...]) for mixed in mixeds]
    for rs, x, out in zip(groups, xs, outs):
        h_ref[rs, :] = _layer_norm(DEEPNORM_ALPHA * x + out, g_ref[...], b_ref[...])


def _merge(x2, o_dn, o_mla, wzg, wdn, wmla, wo, nw, g, b):
    t = x2.shape[0]
    tm = MERGE_TM
    row = lambda i: (i, 0)
    act = pl.BlockSpec((tm, D_MODEL), row)
    return pl.pallas_call(
        _merge_kernel,
        grid=(t // tm,),
        in_specs=[act, act, act, _const_spec(wzg.shape), _const_spec(wdn.shape), _const_spec(wmla.shape),
                  _const_spec(wo.shape), _const_spec(nw.shape), _const_spec(g.shape), _const_spec(b.shape)],
        out_specs=act,
        out_shape=jax.ShapeDtypeStruct((t, D_MODEL), F32),
        compiler_params=_params(1),
        name="merge",
    )(x2, o_dn, o_mla, wzg, wdn, wmla, wo, nw, g, b)


FFN_TM = 1024
FFN_SLAB = 256


def _ffn_kernel(h_ref, p_ref, win_ref, wout_ref, wpg_ref, wple_ref, g_ref, b_ref, o_ref):
    h = h_ref[...]
    hb = h.astype(BF16)
    acc = jnp.zeros(h.shape, F32)
    for j in range(FFN_HIDDEN // FFN_SLAB):
        cs = slice(j * FFN_SLAB, (j + 1) * FFN_SLAB)
        gt = _dot(hb, win_ref[:, cs])
        up = _dot(hb, win_ref[:, FFN_HIDDEN + j * FFN_SLAB:FFN_HIDDEN + (j + 1) * FFN_SLAB])
        a = (_silu(gt) * up).astype(BF16)
        acc = acc + _dot(a, wout_ref[cs, :])
    ple = _sigmoid(_dot(hb, wpg_ref[...])) * _dot(p_ref[...].astype(BF16), wple_ref[...])
    r = DEEPNORM_ALPHA * h + acc + ple
    o_ref[...] = _layer_norm(r, g_ref[...], b_ref[...])


def _ffn(h1, p2, win, wout, wpg, wple, g, b):
    assert FFN_HIDDEN % FFN_SLAB == 0
    t = h1.shape[0]
    tm = FFN_TM
    row = lambda i: (i, 0)
    act = pl.BlockSpec((tm, D_MODEL), row)
    return pl.pallas_call(
        _ffn_kernel,
        grid=(t // tm,),
        in_specs=[act, pl.BlockSpec((tm, PLE_DIM), row), _const_spec(win.shape), _const_spec(wout.shape),
                  _const_spec(wpg.shape), _const_spec(wple.shape), _const_spec(g.shape), _const_spec(b.shape)],
        out_specs=act,
        out_shape=jax.ShapeDtypeStruct((t, D_MODEL), F32),
        compiler_params=_params(1),
        name="ffn",
    )(h1, p2, win, wout, wpg, wple, g, b)


def _split_w_in(w_in):
    offs = [0]
    for n in IN_SIZES:
        offs.append(offs[-1] + n)
    return [w_in[:, offs[i]:offs[i + 1]] for i in range(len(IN_SIZES))]


def kernel(x, p, positions, w_in, conv_w, dn_a_log, dn_dt_bias, dn_norm_w, q_norm_w, w_uq, kv_norm_w, w_uk, w_uv,
           w_br_dn, w_br_mla, w_o, ln1_g, ln1_b, w_ffn_in, w_ffn_out, w_ple, w_ple_gate, ln2_g, ln2_b):
    assert w_in.shape[0] == DEPTH
    batch, seq, _ = x.shape
    t = batch * seq
    assert x.shape[2] == D_MODEL and batch % SCAN_BATCH == 0
    assert all(seq % n == 0 for n in (QKV_TM, PREP_ROWS, SCAN_ROWS, ATT_TQ, ATT_TK)) and ATT_TQ == ATT_TK
    assert t % MERGE_TM == 0 and t % FFN_TM == 0 and MERGE_TM % MERGE_ROWS == 0
    h = x.reshape(t, D_MODEL)
    pos2 = positions.reshape(t, 1)
    half = MLA_ROPE // 2
    invf = ROPE_BASE ** (-jnp.arange(0, MLA_ROPE, 2, dtype=F32) / MLA_ROPE)
    invf = jnp.tile(invf, LANES // half).reshape(1, LANES)

    for i in range(DEPTH):
        w_in_bf = w_in[i].astype(BF16)
        _, w_z, w_b, w_a, w_cq, w_ckv, w_kr, w_gdn, w_gmla = _split_w_in(w_in_bf)
        w_kr_sw = jnp.concatenate([w_kr[:, half:], w_kr[:, :half]], axis=1)
        pad = jnp.zeros((D_MODEL, LAT_W - LAT_BA - 2 * DN_HEADS), BF16)
        w_lat = jnp.concatenate([w_cq, w_ckv, w_kr, w_kr_sw, w_b, w_a, pad], axis=1)
        w_zg = jnp.concatenate([w_z, w_gdn, w_gmla], axis=1)
        uq = w_uq[i]
        t1 = uq[:, :, MLA_NOPE:MLA_NOPE + half]
        t2 = uq[:, :, MLA_NOPE + half:]
        wqn = uq[:, :, :MLA_NOPE].reshape(MLA_Q_LORA, MLA_HEADS * MLA_NOPE).astype(BF16)
        wqr = jnp.concatenate([t1, t2, t2, t1], axis=2).reshape(MLA_Q_LORA, MLA_HEADS * LANES).astype(BF16)
        wuk = w_uk[i].reshape(MLA_KV_LORA, MLA_HEADS * MLA_NOPE).astype(BF16)
        wuv = w_uv[i].reshape(MLA_KV_LORA, MLA_HEADS * MLA_V).astype(BF16)
        lane_pad = jnp.zeros((LANES - 2 * DN_HEADS,), F32)
        head_pad = jnp.zeros((DN_HEADS,), F32)
        dnp = jnp.stack([jnp.concatenate([head_pad, dn_a_log[i], lane_pad]),
                         jnp.concatenate([head_pad, dn_dt_bias[i], lane_pad])])
        nw = jnp.tile(dn_norm_w[i], 1).reshape(1, DN_DV)
        w_ffn = w_ffn_in[i].astype(BF16)

        q, k, v, qn, qr, kn, kr, vm, bg = _input_stage(
            h, w_in_bf, conv_w[i], w_lat, pos2, invf, q_norm_w[i].reshape(1, -1), kv_norm_w[i].reshape(1, -1),
            wqn, wqr, wuk, wuv, dnp, seq)
        w, qd, kt, u, intra, egl = _dn_prep(q, k, v, bg)
        o_dn = _dn_scan(w, qd, kt, u, intra, egl, batch, seq)
        o_mla = _mla_attn(qn, qr, kn, kr, vm, batch, seq)
        h1 = _merge(h, o_dn, o_mla, w_zg, w_br_dn[i].astype(BF16), w_br_mla[i].astype(BF16),
                    w_o[i].astype(BF16), nw, ln1_g[i].reshape(1, -1), ln1_b[i].reshape(1, -1))
        h = _ffn(h1, p[i].reshape(t, PLE_DIM), w_ffn,
                 w_ffn_out[i].astype(BF16), w_ple_gate[i].astype(BF16), w_ple[i].astype(BF16),
                 ln2_g[i].reshape(1, -1), ln2_b[i].reshape(1, -1))
    return h.reshape(batch, seq, D_MODEL)
```
